```python
import math
import jax, jax.numpy as jnp
from jax import lax
import numpy as np

D_MODEL = 1024
BATCH = 2
SEQ = 8192
DEPTH = 2

D_MIX = D_MODEL
HEAD_DIM = 64
N_HEADS = 8
N_KV_HEADS = 2
Q_PER_KV = N_HEADS // N_KV_HEADS
D_ATTN = N_HEADS * HEAD_DIM
D_KV = N_KV_HEADS * HEAD_DIM
WINDOW = 128
BLOCK = 128
N_GM_GROUPS = 8
GM_GROUP_DIM = 64
D_GM = N_GM_GROUPS * GM_GROUP_DIM
CHUNK = 128
D_IN = D_ATTN + 2 * D_KV + D_ATTN + 3 * D_GM
EPS = 1e-6
NEG_INF = -1e30

kernel_name = "hymba_style_bidir_attn_gmlp_hybrid"


def _rms(x, eps=EPS):
    xf = x.astype(jnp.float32)
    return (xf * lax.rsqrt(jnp.mean(xf * xf, axis=-1, keepdims=True) + eps)).astype(x.dtype)


def _alibi_slopes(n_heads):
    return 2.0 ** (-8.0 * jnp.arange(1, n_heads + 1, dtype=jnp.float32) / n_heads)


def _windowed_gqa(q, k, v, q_gain, k_gain, sink):
    b, s = q.shape[0], q.shape[1]
    nb = s // BLOCK
    q = _rms(q) * q_gain
    k = _rms(k) * k_gain
    qb = q.reshape(b, nb, BLOCK, N_KV_HEADS, Q_PER_KV, HEAD_DIM)
    pad = ((0, 0), (BLOCK, BLOCK), (0, 0), (0, 0))
    kp = jnp.pad(k, pad).reshape(b, nb + 2, BLOCK, N_KV_HEADS, HEAD_DIM)
    vp = jnp.pad(v, pad).reshape(b, nb + 2, BLOCK, N_KV_HEADS, HEAD_DIM)
    kb = jnp.concatenate([kp[:, :-2], kp[:, 1:-1], kp[:, 2:]], axis=2)
    vb = jnp.concatenate([vp[:, :-2], vp[:, 1:-1], vp[:, 2:]], axis=2)
    scores = jnp.einsum('bnqkgd,bnskd->bnkgqs', qb, kb).astype(jnp.float32) / math.sqrt(HEAD_DIM)
    blk = jnp.arange(nb)[:, None, None]
    qpos = blk * BLOCK + jnp.arange(BLOCK)[None, :, None]
    kpos = blk * BLOCK - BLOCK + jnp.arange(3 * BLOCK)[None, None, :]
    dist = jnp.abs(kpos - qpos).astype(jnp.float32)
    valid = (dist <= WINDOW) & (kpos >= 0) & (kpos < s)
    slopes = _alibi_slopes(N_HEADS).reshape(N_KV_HEADS, Q_PER_KV)
    scores = scores - slopes[None, None, :, :, None, None] * dist[None, :, None, None]
    scores = jnp.where(valid[None, :, None, None], scores, NEG_INF)
    sink_col = jnp.broadcast_to(
        sink.astype(jnp.float32).reshape(N_KV_HEADS, Q_PER_KV)[None, None, :, :, None, None],
        scores.shape[:-1] + (1,))
    probs = jax.nn.softmax(jnp.concatenate([scores, sink_col], axis=-1), axis=-1)[..., :-1]
    out = jnp.einsum('bnkgqs,bnskd->bnqkgd', probs.astype(v.dtype), vb)
    return out.reshape(b, s, D_ATTN)


def _chunked_gmlp(u, vg, w_s, b_s):
    b, s = u.shape[0], u.shape[1]
    nc = s // CHUNK
    vn = _rms(vg.reshape(b, nc, CHUNK, N_GM_GROUPS, GM_GROUP_DIM))
    sv = jnp.einsum('gts,bcsge->bctge', w_s, vn) + b_s.T[None, None, :, :, None]
    return u * sv.reshape(b, s, D_GM)


def setup_inputs(seed: int = 0) -> dict:
    key = jax.random.key(seed)
    ks = jax.random.split(key, 12)
    f32 = jnp.float32
    x = jax.random.normal(ks[0], (BATCH, SEQ, D_MODEL), f32)
    c = jax.random.normal(ks[1], (BATCH, D_MODEL), f32)
    w_ada = jax.random.normal(ks[2], (DEPTH, D_MODEL, 3 * D_MODEL), f32) * D_MODEL ** -0.5
    b_ada = jax.random.normal(ks[3], (DEPTH, 3 * D_MODEL), f32) * 0.02
    norm_gain = 1.0 + 0.01 * jax.random.normal(ks[4], (DEPTH, D_MODEL), f32)
    w_in = jax.random.normal(ks[5], (DEPTH, D_MODEL, D_IN), f32) * D_MODEL ** -0.5
    q_gain = 1.0 + 0.01 * jax.random.normal(ks[6], (DEPTH, HEAD_DIM), f32)
    k_gain = 1.0 + 0.01 * jax.random.normal(ks[7], (DEPTH, HEAD_DIM), f32)
    sink = jax.random.normal(ks[8], (DEPTH, N_HEADS), f32) * 0.5
    w_s = jax.random.normal(ks[9], (DEPTH, N_GM_GROUPS, CHUNK, CHUNK), f32) * (0.5 * CHUNK ** -0.5)
    b_s = 1.0 + 0.01 * jax.random.normal(ks[10], (DEPTH, N_GM_GROUPS, CHUNK), f32)
    w_out = jax.random.normal(ks[11], (DEPTH, D_MIX, D_MODEL), f32) * D_MIX ** -0.5
    return {"x": x, "c": c, "w_ada": w_ada, "b_ada": b_ada, "norm_gain": norm_gain,
            "w_in": w_in, "q_gain": q_gain, "k_gain": k_gain, "sink": sink,
            "w_s": w_s, "b_s": b_s, "w_out": w_out}


def reference(x, c, w_ada, b_ada, norm_gain, w_in, q_gain, k_gain, sink, w_s, b_s, w_out):
    b, s, _ = x.shape
    cond = jax.nn.silu(c)
    splits = np.cumsum([D_ATTN, D_KV, D_KV, D_ATTN, D_GM, D_GM])
    for l in range(DEPTH):
        ada = cond @ w_ada[l] + b_ada[l]
        shift, scale, gate = jnp.split(ada, 3, axis=-1)
        h = _rms(x) * norm_gain[l]
        h = h * (1.0 + scale[:, None, :]) + shift[:, None, :]
        proj = h @ w_in[l]
        q, k, v, g_attn, u, v_gm, g_gm = jnp.split(proj, splits, axis=-1)
        attn = _windowed_gqa(q.reshape(b, s, N_HEADS, HEAD_DIM),
                             k.reshape(b, s, N_KV_HEADS, HEAD_DIM),
                             v.reshape(b, s, N_KV_HEADS, HEAD_DIM),
                             q_gain[l], k_gain[l], sink[l])
        gm = _chunked_gmlp(u, v_gm, w_s[l], b_s[l])
        y = jnp.concatenate([attn * jax.nn.silu(g_attn), gm * jax.nn.silu(g_gm)], axis=-1)
        x = x + gate[:, None, :] * (y @ w_out[l])
    return x
```

```python
import functools

import jax
import jax.numpy as jnp
from jax import lax
from jax.experimental import pallas as pl
from jax.experimental.pallas import tpu as pltpu

D_MODEL = 1024
N_HEADS = 8
N_KV_HEADS = 2
Q_PER_KV = N_HEADS // N_KV_HEADS
HEAD_DIM = 64
D_ATTN = N_HEADS * HEAD_DIM
D_KV = N_KV_HEADS * HEAD_DIM
BLOCK = 128
N_GM_GROUPS = 8
GM_GROUP_DIM = 64
D_GM = N_GM_GROUPS * GM_GROUP_DIM
D_IN = D_ATTN + 2 * D_KV + D_ATTN + 3 * D_GM
EPS = 1e-6
NEG_INF = -1e30

Q0, K0, V0, GA0, U0, VG0, GG0 = 0, 512, 640, 768, 1280, 1792, 2304

TILE = 512
BLOCKS_PER_TILE = TILE // BLOCK
KEYS = 3 * BLOCK
VMEM_LIMIT_BYTES = 52 * 1024 * 1024

F32 = jnp.float32
BF16 = jnp.bfloat16


def _dot_nt(a, b):
    return lax.dot_general(a, b, (((1,), (1,)), ((), ())), preferred_element_type=F32)


def _dot_tn(a, b):
    return lax.dot_general(a, b, (((0,), (0,)), ((), ())), preferred_element_type=F32)


def _silu(v):
    return v * (1.0 / (1.0 + jnp.exp(-v)))


def _ada_kernel(c_ref, w_ref, b_ref, o_ref):
    cond = _silu(c_ref[...])
    o_ref[...] = jnp.dot(cond, w_ref[...], preferred_element_type=F32) + b_ref[...]


def _ada(c, w_ada, b_ada):
    depth = w_ada.shape[0]
    bsz = c.shape[0]
    rows = 8
    c_pad = jnp.pad(c, ((0, rows - bsz), (0, 0)))
    tn = 1024
    out = pl.pallas_call(
        _ada_kernel,
        grid=(depth, 3 * D_MODEL // tn),
        in_specs=[
            pl.BlockSpec((rows, D_MODEL), lambda l, n: (0, 0)),
            pl.BlockSpec((None, D_MODEL, tn), lambda l, n: (l, 0, n)),
            pl.BlockSpec((None, 1, tn), lambda l, n: (l, 0, n)),
        ],
        out_specs=pl.BlockSpec((None, rows, tn), lambda l, n: (l, 0, n)),
        out_shape=jax.ShapeDtypeStruct((depth, rows, 3 * D_MODEL), F32),
        compiler_params=pltpu.CompilerParams(
            dimension_semantics=("arbitrary", "arbitrary")),
        name="ada",
    )(c_pad, w_ada, b_ada.reshape(depth, 1, 3 * D_MODEL))
    return out[:, :bsz]


def _layer_kernel(n_tiles,
                  sink_ref, x_ref, xprev_ref, ada_ref, ng_ref, w_in_t_ref, qg_ref, kg_ref,
                  w_s_ref, b_s_ref, w_out_ref,
                  o_ref,
                  h_ref, qT_ref, sgaT_ref, yT_ref, k_ref, vT_ref, bias_ref):
    b = pl.program_id(0)
    j = pl.program_id(1)
    T = TILE

    @pl.when((b == 0) & (j == 0))
    def _init_bias():
        s_idx = lax.broadcasted_iota(jnp.int32, (KEYS, BLOCK), 0)
        q_idx = lax.broadcasted_iota(jnp.int32, (KEYS, BLOCK), 1)
        dist = jnp.abs(s_idx - BLOCK - q_idx).astype(F32)
        valid = dist <= float(BLOCK)
        for kvh in range(N_KV_HEADS):
            for g in range(Q_PER_KV):
                slope = 2.0 ** (-8.0 * (kvh * Q_PER_KV + g + 1) / N_HEADS)
                bias_ref[kvh, :, g * BLOCK:(g + 1) * BLOCK] = jnp.where(valid, -slope * dist, NEG_INF)

    ada = ada_ref[...]
    shift = ada[:, 0:D_MODEL]
    scale = ada[:, D_MODEL:2 * D_MODEL]
    gate = ada[:, 2 * D_MODEL:3 * D_MODEL]

    @pl.when(j < n_tiles)
    def _phase_a():
        @pl.when(j == 0)
        def _zero_halo():
            k_ref[0:BLOCK, :] = jnp.zeros((BLOCK, D_KV), BF16)
            vT_ref[:, 0:BLOCK] = jnp.zeros((D_KV, BLOCK), BF16)

        x = x_ref[...]
        ms = jnp.mean(x * x, axis=-1, keepdims=True)
        a = ng_ref[...] * (1.0 + scale)
        h = ((x * lax.rsqrt(ms + EPS)) * a + shift).astype(BF16)
        h_ref[...] = h
        kvT = _dot_nt(w_in_t_ref[K0:GA0, :], h)
        kn = []
        for kvh in range(N_KV_HEADS):
            kh = kvT[kvh * HEAD_DIM:(kvh + 1) * HEAD_DIM]
            ss = jnp.sum(kh * kh, axis=0, keepdims=True) * (1.0 / HEAD_DIM)
            kn.append(kh * lax.rsqrt(ss + EPS) * kg_ref[...])
        k_nat = jnp.concatenate(kn, axis=0).T
        k_ref[BLOCK + T:BLOCK + 2 * T, :] = k_nat.astype(BF16)
        vT_ref[:, BLOCK + T:BLOCK + 2 * T] = kvT[D_KV:2 * D_KV].astype(BF16)

    @pl.when(j > 0)
    def _phase_b():
        tile = j - 1
        edge_lo = jnp.where(tile == 0, NEG_INF, 0.0).astype(F32)
        edge_hi = jnp.where(tile == n_tiles - 1, NEG_INF, 0.0).astype(F32)
        zeros_q = jnp.zeros((HEAD_DIM, Q_PER_KV * BLOCK), BF16)
        for blk in range(BLOCKS_PER_TILE):
            cols = slice(blk * BLOCK, (blk + 1) * BLOCK)
            k_blk = k_ref[blk * BLOCK:blk * BLOCK + KEYS, :]
            for kvh in range(N_KV_HEADS):
                heads = [kvh * Q_PER_KV + g for g in range(Q_PER_KV)]
                q_cat = jnp.concatenate(
                    [qT_ref[hd * HEAD_DIM:(hd + 1) * HEAD_DIM, cols] for hd in heads], axis=1)
                rhs = jnp.concatenate([q_cat, zeros_q] if kvh == 0 else [zeros_q, q_cat], axis=0)
                s = jnp.dot(k_blk, rhs, preferred_element_type=F32)
                s = s + bias_ref[kvh]
                if blk == 0:
                    s = jnp.concatenate([s[0:BLOCK] + edge_lo, s[BLOCK:]], axis=0)
                if blk == BLOCKS_PER_TILE - 1:
                    s = jnp.concatenate([s[:2 * BLOCK], s[2 * BLOCK:] + edge_hi], axis=0)
                sink_row = jnp.concatenate(
                    [jnp.full((1, BLOCK), sink_ref[hd], F32) for hd in heads], axis=1)
                m = jnp.maximum(jnp.max(s, axis=0, keepdims=True), sink_row)
                p = jnp.exp(s - m)
                l = jnp.sum(p, axis=0, keepdims=True) + jnp.exp(sink_row - m)
                vT_blk = vT_ref[kvh * HEAD_DIM:(kvh + 1) * HEAD_DIM,
                                blk * BLOCK:blk * BLOCK + KEYS]
                o = jnp.dot(vT_blk, p.astype(BF16), preferred_element_type=F32)
                o = o * (1.0 / l)
                for g, hd in enumerate(heads):
                    rows = slice(hd * HEAD_DIM, (hd + 1) * HEAD_DIM)
                    yT_ref[rows, cols] = (o[:, g * BLOCK:(g + 1) * BLOCK]
                                          * sgaT_ref[rows, cols]).astype(BF16)
        y_out = _dot_tn(yT_ref[...], w_out_ref[...])
        o_ref[...] = xprev_ref[...] + gate * y_out

    @pl.when(j < n_tiles)
    def _phase_c():
        h = h_ref[...]
        qT = _dot_nt(w_in_t_ref[Q0:K0, :], h)
        for hd in range(N_HEADS):
            rows = slice(hd * HEAD_DIM, (hd + 1) * HEAD_DIM)
            qh = qT[rows]
            ss = jnp.sum(qh * qh, axis=0, keepdims=True) * (1.0 / HEAD_DIM)
            rstd = lax.rsqrt(ss + EPS) * (1.0 / 8.0)
            qT_ref[rows, :] = (qh * rstd * qg_ref[...]).astype(BF16)
        sgaT_ref[...] = _silu(_dot_nt(w_in_t_ref[GA0:U0, :], h))
        uT = _dot_nt(w_in_t_ref[U0:VG0, :], h)
        vgT = _dot_nt(w_in_t_ref[VG0:GG0, :], h)
        ggT = _dot_nt(w_in_t_ref[GG0:D_IN, :], h)
        for grp in range(N_GM_GROUPS):
            rows = slice(grp * GM_GROUP_DIM, (grp + 1) * GM_GROUP_DIM)
            vg = vgT[rows]
            ss = jnp.sum(vg * vg, axis=0, keepdims=True) * (1.0 / GM_GROUP_DIM)
            vn = (vg * lax.rsqrt(ss + EPS)).astype(BF16)
            lhs = jnp.concatenate(
                [vn[:, c * BLOCK:(c + 1) * BLOCK] for c in range(BLOCKS_PER_TILE)], axis=0)
            sv = _dot_nt(lhs, w_s_ref[grp]) + b_s_ref[grp:grp + 1, :]
            svT = jnp.concatenate(
                [sv[c * GM_GROUP_DIM:(c + 1) * GM_GROUP_DIM] for c in range(BLOCKS_PER_TILE)],
                axis=1)
            yT_ref[D_ATTN + grp * GM_GROUP_DIM:D_ATTN + (grp + 1) * GM_GROUP_DIM, :] = (
                uT[rows] * svT * _silu(ggT[rows])).astype(BF16)

        @pl.when(j > 0)
        def _halo():
            k_ref[0:BLOCK, :] = k_ref[T:T + BLOCK, :]
            vT_ref[:, 0:BLOCK] = vT_ref[:, T:T + BLOCK]

        k_ref[BLOCK:BLOCK + T, :] = k_ref[BLOCK + T:BLOCK + 2 * T, :]
        vT_ref[:, BLOCK:BLOCK + T] = vT_ref[:, BLOCK + T:BLOCK + 2 * T]


def _layer(x, ada_l, norm_gain_l, w_in_t_l, q_gain_l, k_gain_l, sink_l, w_s_l, b_s_l, w_out_l):
    bsz, seq, _ = x.shape
    T = TILE
    n_tiles = seq // T
    const2 = lambda b, j: (0, 0)
    const3 = lambda b, j: (0, 0, 0)
    return pl.pallas_call(
        functools.partial(_layer_kernel, n_tiles),
        grid=(bsz, n_tiles + 1),
        in_specs=[
            pl.BlockSpec(memory_space=pltpu.SMEM),
            pl.BlockSpec((None, T, D_MODEL), lambda b, j: (b, jnp.minimum(j, n_tiles - 1), 0)),
            pl.BlockSpec((None, T, D_MODEL), lambda b, j: (b, jnp.maximum(j - 1, 0), 0)),
            pl.BlockSpec((None, 1, 3 * D_MODEL), lambda b, j: (b, 0, 0)),
            pl.BlockSpec((1, D_MODEL), const2),
            pl.BlockSpec((D_IN, D_MODEL), const2),
            pl.BlockSpec((HEAD_DIM, 1), const2),
            pl.BlockSpec((HEAD_DIM, 1), const2),
            pl.BlockSpec((N_GM_GROUPS, BLOCK, BLOCK), const3),
            pl.BlockSpec((N_GM_GROUPS, BLOCK), const2),
            pl.BlockSpec((D_MODEL, D_MODEL), const2),
        ],
        out_specs=pl.BlockSpec((None, T, D_MODEL), lambda b, j: (b, jnp.maximum(j - 1, 0), 0)),
        out_shape=jax.ShapeDtypeStruct(x.shape, F32),
        scratch_shapes=[
            pltpu.VMEM((T, D_MODEL), BF16),
            pltpu.VMEM((D_ATTN, T), BF16),
            pltpu.VMEM((D_ATTN, T), F32),
            pltpu.VMEM((D_MODEL, T), BF16),
            pltpu.VMEM((BLOCK + 2 * T, D_KV), BF16),
            pltpu.VMEM((D_KV, BLOCK + 2 * T), BF16),
            pltpu.VMEM((N_KV_HEADS, KEYS, Q_PER_KV * BLOCK), F32),
        ],
        compiler_params=pltpu.CompilerParams(
            dimension_semantics=("arbitrary", "arbitrary"),
            vmem_limit_bytes=VMEM_LIMIT_BYTES),
        name="mixer_layer",
    )(sink_l, x, x, ada_l, norm_gain_l, w_in_t_l, q_gain_l, k_gain_l, w_s_l, b_s_l, w_out_l)


def kernel(x, c, w_ada, b_ada, norm_gain, w_in, q_gain, k_gain, sink, w_s, b_s, w_out):
    depth = w_in.shape[0]
    bsz = x.shape[0]
    ada = _ada(c, w_ada, b_ada).reshape(depth, bsz, 1, 3 * D_MODEL)
    w_in_t = jnp.swapaxes(w_in, 1, 2).astype(BF16)
    w_s_b = w_s.astype(BF16)
    w_out_b = w_out.astype(BF16)
    for l in range(depth):
        x = _layer(x, ada[l], norm_gain[l].reshape(1, D_MODEL), w_in_t[l],
                   q_gain[l].reshape(HEAD_DIM, 1), k_gain[l].reshape(HEAD_DIM, 1),
                   sink[l], w_s_b[l], b_s[l], w_out_b[l])
    return x
```

```python
import functools
import math

import jax
import jax.numpy as jnp
from jax import lax
from jax.experimental import pallas as pl
from jax.experimental.pallas import tpu as pltpu

D_MODEL = 1024
N_HEADS = 8
N_KV_HEADS = 2
Q_PER_KV = N_HEADS // N_KV_HEADS
HEAD_DIM = 64
D_ATTN = N_HEADS * HEAD_DIM
D_KV = N_KV_HEADS * HEAD_DIM
BLOCK = 128
N_GM_GROUPS = 8
GM_GROUP_DIM = 64
D_GM = N_GM_GROUPS * GM_GROUP_DIM
D_IN = D_ATTN + 2 * D_KV + D_ATTN + 3 * D_GM
EPS = 1e-6
NEG_INF = -1e30
LOG2E = math.log2(math.e)

Q0 = 0
K0 = Q0 + D_ATTN
V0 = K0 + D_KV
GA0 = V0 + D_KV
U0 = GA0 + D_ATTN
VG0 = U0 + D_GM
GG0 = VG0 + D_GM
assert GG0 + D_GM == D_IN and N_HEADS == N_GM_GROUPS and HEAD_DIM == GM_GROUP_DIM

TILE = 512
BLOCKS_PER_TILE = TILE // BLOCK
KEYS = 3 * BLOCK
VMEM_LIMIT_BYTES = 52 * 1024 * 1024

F32 = jnp.float32
BF16 = jnp.bfloat16


def _dot_nt(a, b):
    return lax.dot_general(a, b, (((1,), (1,)), ((), ())), preferred_element_type=F32)


def _dot_tn(a, b):
    return lax.dot_general(a, b, (((0,), (0,)), ((), ())), preferred_element_type=F32)


def _silu(v):
    return v * (1.0 / (1.0 + jnp.exp(-v)))


def _ada_kernel(c_ref, w_ref, b_ref, o_ref):
    cond = _silu(c_ref[...])
    o_ref[...] = jnp.dot(cond, w_ref[...], preferred_element_type=F32) + b_ref[...]


def _ada(c, w_ada, b_ada):
    depth = w_ada.shape[0]
    bsz = c.shape[0]
    rows = 8
    c_pad = jnp.pad(c, ((0, rows - bsz), (0, 0)))
    tn = D_MODEL
    out = pl.pallas_call(
        _ada_kernel,
        grid=(depth, 3 * D_MODEL // tn),
        in_specs=[
            pl.BlockSpec((rows, D_MODEL), lambda l, n: (0, 0)),
            pl.BlockSpec((None, D_MODEL, tn), lambda l, n: (l, 0, n)),
            pl.BlockSpec((None, 1, tn), lambda l, n: (l, 0, n)),
        ],
        out_specs=pl.BlockSpec((None, rows, tn), lambda l, n: (l, 0, n)),
        out_shape=jax.ShapeDtypeStruct((depth, rows, 3 * D_MODEL), F32),
        compiler_params=pltpu.CompilerParams(
            dimension_semantics=("arbitrary", "arbitrary")),
        name="ada",
    )(c_pad, w_ada, b_ada.reshape(depth, 1, 3 * D_MODEL))
    return out[:, :bsz]


def _layer_kernel(n_tiles,
                  sink_ref, x_ref, xprev_ref, ada_ref, ng_ref, w_in_t_ref, qg_ref, kg_ref,
                  w_s_ref, b_s_ref, w_out_ref,
                  o_ref,
                  h_ref, qT_new, qT_old, sga_new, sga_old, yaT_ref, ygT_ref, k_ref, vT_ref,
                  bias_ref):
    b = pl.program_id(0)
    j = pl.program_id(1)
    T = TILE

    @pl.when((b == 0) & (j == 0))
    def _init_bias():
        s_idx = lax.broadcasted_iota(jnp.int32, (KEYS, BLOCK), 0)
        q_idx = lax.broadcasted_iota(jnp.int32, (KEYS, BLOCK), 1)
        dist = jnp.abs(s_idx - BLOCK - q_idx).astype(F32)
        valid = dist <= float(BLOCK)
        for kvh in range(N_KV_HEADS):
            for g in range(Q_PER_KV):
                slope = 2.0 ** (-8.0 * (kvh * Q_PER_KV + g + 1) / N_HEADS)
                bias_ref[kvh, :, g * BLOCK:(g + 1) * BLOCK] = jnp.where(
                    valid, (-slope * LOG2E) * dist, NEG_INF)

    ada = ada_ref[...]
    shift = ada[:, 0:D_MODEL]
    scale = ada[:, D_MODEL:2 * D_MODEL]
    gate = ada[:, 2 * D_MODEL:3 * D_MODEL]

    def norm_and_kv():
        x = x_ref[...]
        ms = jnp.mean(x * x, axis=-1, keepdims=True)
        a = ng_ref[...] * (1.0 + scale)
        h_ref[...] = ((x * lax.rsqrt(ms + EPS)) * a + shift).astype(BF16)
        kvT = _dot_nt(w_in_t_ref[K0:GA0, :], h_ref[...])
        kn = []
        for kvh in range(N_KV_HEADS):
            kh = kvT[kvh * HEAD_DIM:(kvh + 1) * HEAD_DIM]
            ss = jnp.sum(kh * kh, axis=0, keepdims=True) * (1.0 / HEAD_DIM)
            kn.append(kh * lax.rsqrt(ss + EPS) * kg_ref[...])
        k_nat = jnp.concatenate(kn, axis=0).T
        k_ref[BLOCK + T:BLOCK + 2 * T, :] = k_nat.astype(BF16)
        vT_ref[:, BLOCK + T:BLOCK + 2 * T] = kvT[D_KV:2 * D_KV].astype(BF16)

    def project_piece(i):
        rows = slice(i * HEAD_DIM, (i + 1) * HEAD_DIM)
        lhs = jnp.concatenate(
            [w_in_t_ref[off + i * HEAD_DIM:off + (i + 1) * HEAD_DIM, :]
             for off in (Q0, GA0, U0, VG0, GG0)], axis=0)
        r = _dot_nt(lhs, h_ref[...])
        qh, ga, u, vg, gg = (r[n * HEAD_DIM:(n + 1) * HEAD_DIM] for n in range(5))
        ss = jnp.sum(qh * qh, axis=0, keepdims=True) * (1.0 / HEAD_DIM)
        rstd = lax.rsqrt(ss + EPS) * (LOG2E / math.sqrt(HEAD_DIM))
        qT_new[rows, :] = (qh * rstd * qg_ref[...]).astype(BF16)
        sga_new[rows, :] = _silu(ga)
        ss = jnp.sum(vg * vg, axis=0, keepdims=True) * (1.0 / GM_GROUP_DIM)
        vn = (vg * lax.rsqrt(ss + EPS)).astype(BF16)
        vn_st = jnp.concatenate(
            [vn[:, c * BLOCK:(c + 1) * BLOCK] for c in range(BLOCKS_PER_TILE)], axis=0)
        sv = _dot_nt(vn_st, w_s_ref[i]) + b_s_ref[i:i + 1, :]
        svT = jnp.concatenate(
            [sv[c * GM_GROUP_DIM:(c + 1) * GM_GROUP_DIM] for c in range(BLOCKS_PER_TILE)],
            axis=1)
        ygT_ref[rows, :] = (u * svT * _silu(gg)).astype(BF16)

    def unit_scores(blk, kvh, edge_lo, edge_hi):
        cols = slice(blk * BLOCK, (blk + 1) * BLOCK)
        heads = [kvh * Q_PER_KV + g for g in range(Q_PER_KV)]
        k_blk = k_ref[blk * BLOCK:blk * BLOCK + KEYS, :]
        q_cat = jnp.concatenate(
            [qT_old[hd * HEAD_DIM:(hd + 1) * HEAD_DIM, cols] for hd in heads], axis=1)
        zeros_q = jnp.zeros((HEAD_DIM, Q_PER_KV * BLOCK), BF16)
        rhs = jnp.concatenate([q_cat, zeros_q] if kvh == 0 else [zeros_q, q_cat], axis=0)
        s = jnp.dot(k_blk, rhs, preferred_element_type=F32)
        s = s + bias_ref[kvh]
        if blk == 0:
            s = jnp.concatenate([s[0:BLOCK] + edge_lo, s[BLOCK:]], axis=0)
        if blk == BLOCKS_PER_TILE - 1:
            s = jnp.concatenate([s[:2 * BLOCK], s[2 * BLOCK:] + edge_hi], axis=0)
        return s

    def unit_finish(blk, kvh, s):
        cols = slice(blk * BLOCK, (blk + 1) * BLOCK)
        heads = [kvh * Q_PER_KV + g for g in range(Q_PER_KV)]
        sink_row = jnp.concatenate(
            [jnp.full((1, BLOCK), sink_ref[hd] * LOG2E, F32) for hd in heads], axis=1)
        m = jnp.maximum(jnp.max(s, axis=0, keepdims=True), sink_row)
        p = jnp.exp2(s - m)
        l = jnp.sum(p, axis=0, keepdims=True) + jnp.exp2(sink_row - m)
        vT_blk = vT_ref[kvh * HEAD_DIM:(kvh + 1) * HEAD_DIM,
                        blk * BLOCK:blk * BLOCK + KEYS]
        o = jnp.dot(vT_blk, p.astype(BF16), preferred_element_type=F32)
        o = o * (1.0 / l)
        for g, hd in enumerate(heads):
            rows = slice(hd * HEAD_DIM, (hd + 1) * HEAD_DIM)
            yaT_ref[rows, cols] = (o[:, g * BLOCK:(g + 1) * BLOCK] * sga_old[rows, cols]).astype(BF16)

    def step(project, attend):
        units = [(blk, kvh) for blk in range(BLOCKS_PER_TILE) for kvh in range(N_KV_HEADS)]
        assert len(units) == N_HEADS
        if attend:
            tile = j - 1
            edge_lo = jnp.where(tile == 0, NEG_INF, 0.0).astype(F32)
            edge_hi = jnp.where(tile == n_tiles - 1, NEG_INF, 0.0).astype(F32)
            o_ref[...] = xprev_ref[...] + gate * _dot_tn(ygT_ref[...], w_out_ref[D_ATTN:, :])
        if project:
            norm_and_kv()
        if attend:
            s_next = unit_scores(*units[0], edge_lo, edge_hi)
        for i, (blk, kvh) in enumerate(units):
            if attend:
                s_cur = s_next
                if i + 1 < len(units):
                    s_next = unit_scores(*units[i + 1], edge_lo, edge_hi)
            if project:
                project_piece(i)
            if attend:
                unit_finish(blk, kvh, s_cur)
        if attend:
            o_ref[...] += gate * _dot_tn(yaT_ref[...], w_out_ref[:D_ATTN, :])

    @pl.when(j == 1)
    def _no_halo():
        k_ref[0:BLOCK, :] = jnp.zeros((BLOCK, D_KV), BF16)
        vT_ref[:, 0:BLOCK] = jnp.zeros((D_KV, BLOCK), BF16)

    @pl.when(j > 1)
    def _halo():
        k_ref[0:BLOCK, :] = k_ref[T:T + BLOCK, :]
        vT_ref[:, 0:BLOCK] = vT_ref[:, T:T + BLOCK]

    @pl.when(j > 0)
    def _carry():
        k_ref[BLOCK:BLOCK + T, :] = k_ref[BLOCK + T:BLOCK + 2 * T, :]
        vT_ref[:, BLOCK:BLOCK + T] = vT_ref[:, BLOCK + T:BLOCK + 2 * T]
        qT_old[...] = qT_new[...]
        sga_old[...] = sga_new[...]

    pl.when(j == 0)(functools.partial(step, True, False))
    pl.when((j > 0) & (j < n_tiles))(functools.partial(step, True, True))
    pl.when(j == n_tiles)(functools.partial(step, False, True))


def _layer(x, ada_l, norm_gain_l, w_in_t_l, q_gain_l, k_gain_l, sink_l, w_s_l, b_s_l, w_out_l):
    bsz, seq, _ = x.shape
    T = TILE
    n_tiles = seq // T
    const2 = lambda b, j: (0, 0)
    const3 = lambda b, j: (0, 0, 0)
    return pl.pallas_call(
        functools.partial(_layer_kernel, n_tiles),
        grid=(bsz, n_tiles + 1),
        in_specs=[
            pl.BlockSpec(memory_space=pltpu.SMEM),
            pl.BlockSpec((None, T, D_MODEL), lambda b, j: (b, jnp.minimum(j, n_tiles - 1), 0)),
            pl.BlockSpec((None, T, D_MODEL), lambda b, j: (b, jnp.maximum(j - 1, 0), 0)),
            pl.BlockSpec((None, 1, 3 * D_MODEL), lambda b, j: (b, 0, 0)),
            pl.BlockSpec((1, D_MODEL), const2),
            pl.BlockSpec((D_IN, D_MODEL), const2),
            pl.BlockSpec((HEAD_DIM, 1), const2),
            pl.BlockSpec((HEAD_DIM, 1), const2),
            pl.BlockSpec((N_GM_GROUPS, BLOCK, BLOCK), const3),
            pl.BlockSpec((N_GM_GROUPS, BLOCK), const2),
            pl.BlockSpec((D_MODEL, D_MODEL), const2),
        ],
        out_specs=pl.BlockSpec((None, T, D_MODEL), lambda b, j: (b, jnp.maximum(j - 1, 0), 0)),
        out_shape=jax.ShapeDtypeStruct(x.shape, F32),
        scratch_shapes=[
            pltpu.VMEM((T, D_MODEL), BF16),
            pltpu.VMEM((D_ATTN, T), BF16),
            pltpu.VMEM((D_ATTN, T), BF16),
            pltpu.VMEM((D_ATTN, T), F32),
            pltpu.VMEM((D_ATTN, T), F32),
            pltpu.VMEM((D_ATTN, T), BF16),
            pltpu.VMEM((D_GM, T), BF16),
            pltpu.VMEM((BLOCK + 2 * T, D_KV), BF16),
            pltpu.VMEM((D_KV, BLOCK + 2 * T), BF16),
            pltpu.VMEM((N_KV_HEADS, KEYS, Q_PER_KV * BLOCK), F32),
        ],
        compiler_params=pltpu.CompilerParams(
            dimension_semantics=("arbitrary", "arbitrary"),
            vmem_limit_bytes=VMEM_LIMIT_BYTES),
        name="mixer_layer",
    )(sink_l, x, x, ada_l, norm_gain_l, w_in_t_l, q_gain_l, k_gain_l, w_s_l, b_s_l, w_out_l)


def kernel(x, c, w_ada, b_ada, norm_gain, w_in, q_gain, k_gain, sink, w_s, b_s, w_out):
    depth = w_in.shape[0]
    bsz = x.shape[0]
    ada = _ada(c, w_ada, b_ada).reshape(depth, bsz, 1, 3 * D_MODEL)
    w_in_t = jnp.swapaxes(w_in, 1, 2).astype(BF16)
    w_s_b = w_s.astype(BF16)
    w_out_b = w_out.astype(BF16)
    for l in range(depth):
        x = _layer(x, ada[l], norm_gain[l].reshape(1, D_MODEL), w_in_t[l],
                   q_gain[l].reshape(HEAD_DIM, 1), k_gain[l].reshape(HEAD_DIM, 1),
                   sink[l], w_s_b[l], b_s[l], w_out_b[l])
    return x
```

```python
import functools
import math

import jax
import jax.numpy as jnp
from jax import lax
from jax.experimental import pallas as pl
from jax.experimental.pallas import tpu as pltpu

D_MODEL = 1024
N_HEADS = 8
N_KV_HEADS = 2
Q_PER_KV = N_HEADS // N_KV_HEADS
HEAD_DIM = 64
D_ATTN = N_HEADS * HEAD_DIM
D_KV = N_KV_HEADS * HEAD_DIM
BLOCK = 128
N_GM_GROUPS = 8
GM_GROUP_DIM = 64
D_GM = N_GM_GROUPS * GM_GROUP_DIM
D_IN = D_ATTN + 2 * D_KV + D_ATTN + 3 * D_GM
EPS = 1e-6
NEG_INF = -1e30
LOG2E = math.log2(math.e)

Q0 = 0
K0 = Q0 + D_ATTN
V0 = K0 + D_KV
GA0 = V0 + D_KV
U0 = GA0 + D_ATTN
VG0 = U0 + D_GM
GG0 = VG0 + D_GM
assert GG0 + D_GM == D_IN and N_HEADS == N_GM_GROUPS and HEAD_DIM == GM_GROUP_DIM

TILE = 512
BLOCKS_PER_TILE = TILE // BLOCK
KEYS = 3 * BLOCK
ADA_ROWS = 8
VMEM_LIMIT_BYTES = 52 * 1024 * 1024

F32 = jnp.float32
BF16 = jnp.bfloat16


def _dot_nt(a, b):
    return lax.dot_general(a, b, (((1,), (1,)), ((), ())), preferred_element_type=F32)


def _dot_tn(a, b):
    return lax.dot_general(a, b, (((0,), (0,)), ((), ())), preferred_element_type=F32)


def _silu(v):
    return v * (1.0 / (1.0 + jnp.exp(-v)))


def _ada_kernel(c_ref, w_ref, b_ref, o_ref):
    cond = _silu(c_ref[...])
    o_ref[...] = jnp.dot(cond, w_ref[...], preferred_element_type=F32) + b_ref[...]


def _ada(c, w_ada, b_ada):
    depth = w_ada.shape[0]
    bsz = c.shape[0]
    c_pad = jnp.pad(c, ((0, ADA_ROWS - bsz), (0, 0)))
    tn = D_MODEL
    return pl.pallas_call(
        _ada_kernel,
        grid=(depth, 3 * D_MODEL // tn),
        in_specs=[
            pl.BlockSpec((ADA_ROWS, D_MODEL), lambda l, n: (0, 0)),
            pl.BlockSpec((None, D_MODEL, tn), lambda l, n: (l, 0, n)),
            pl.BlockSpec((None, 1, tn), lambda l, n: (l, 0, n)),
        ],
        out_specs=pl.BlockSpec((None, ADA_ROWS, tn), lambda l, n: (l, 0, n)),
        out_shape=jax.ShapeDtypeStruct((depth, ADA_ROWS, 3 * D_MODEL), F32),
        compiler_params=pltpu.CompilerParams(
            dimension_semantics=("arbitrary", "arbitrary")),
        name="ada",
    )(c_pad, w_ada, b_ada.reshape(depth, 1, 3 * D_MODEL))


def _layer_kernel(layer, n_tiles,
                  sink_ref, x_ref, xprev_ref, ada_ref, ng_ref, w_in_t_ref, qg_ref, kg_ref,
                  w_s_ref, b_s_ref, w_out_ref,
                  o_ref,
                  h_ref, qT_new, qT_old, sga_new, sga_old, yaT_ref, ygT_ref, k_ref, vT_ref,
                  bias_ref):
    b = pl.program_id(0)
    j = pl.program_id(1)
    T = TILE

    @pl.when((b == 0) & (j == 0))
    def _init_bias():
        s_idx = lax.broadcasted_iota(jnp.int32, (KEYS, BLOCK), 0)
        q_idx = lax.broadcasted_iota(jnp.int32, (KEYS, BLOCK), 1)
        dist = jnp.abs(s_idx - BLOCK - q_idx).astype(F32)
        valid = dist <= float(BLOCK)
        for kvh in range(N_KV_HEADS):
            for g in range(Q_PER_KV):
                slope = 2.0 ** (-8.0 * (kvh * Q_PER_KV + g + 1) / N_HEADS)
                bias_ref[kvh, :, g * BLOCK:(g + 1) * BLOCK] = jnp.where(
                    valid, (-slope * LOG2E) * dist, NEG_INF)

    ada = ada_ref[...]
    shift = ada[:, 0:D_MODEL]
    scale = ada[:, D_MODEL:2 * D_MODEL]
    gate = ada[:, 2 * D_MODEL:3 * D_MODEL]

    def norm_and_kv():
        x = x_ref[...]
        ms = jnp.mean(x * x, axis=-1, keepdims=True)
        a = ng_ref[...] * (1.0 + scale)
        h_ref[...] = ((x * lax.rsqrt(ms + EPS)) * a + shift).astype(BF16)
        kvT = _dot_nt(w_in_t_ref[K0:GA0, :], h_ref[...])
        kn = []
        for kvh in range(N_KV_HEADS):
            kh = kvT[kvh * HEAD_DIM:(kvh + 1) * HEAD_DIM]
            ss = jnp.sum(kh * kh, axis=0, keepdims=True) * (1.0 / HEAD_DIM)
            kn.append(kh * lax.rsqrt(ss + EPS) * kg_ref[...])
        k_nat = jnp.concatenate(kn, axis=0).T
        k_ref[BLOCK + T:BLOCK + 2 * T, :] = k_nat.astype(BF16)
        vT_ref[:, BLOCK + T:BLOCK + 2 * T] = kvT[D_KV:2 * D_KV].astype(BF16)

    def piece_matmul(i):
        lhs = jnp.concatenate(
            [w_in_t_ref[off + i * HEAD_DIM:off + (i + 1) * HEAD_DIM, :]
             for off in (Q0, GA0, U0, VG0, GG0)], axis=0)
        return _dot_nt(lhs, h_ref[...])

    def piece_epilogue(i, r):
        rows = slice(i * HEAD_DIM, (i + 1) * HEAD_DIM)
        qh, ga, u, vg, gg = (r[n * HEAD_DIM:(n + 1) * HEAD_DIM] for n in range(5))
        ss = jnp.sum(qh * qh, axis=0, keepdims=True) * (1.0 / HEAD_DIM)
        rstd = lax.rsqrt(ss + EPS) * (LOG2E / math.sqrt(HEAD_DIM))
        qT_new[rows, :] = (qh * rstd * qg_ref[...]).astype(BF16)
        sga_new[rows, :] = _silu(ga)
        ss = jnp.sum(vg * vg, axis=0, keepdims=True) * (1.0 / GM_GROUP_DIM)
        vn = (vg * lax.rsqrt(ss + EPS)).astype(BF16)
        vn_st = jnp.concatenate(
            [vn[:, c * BLOCK:(c + 1) * BLOCK] for c in range(BLOCKS_PER_TILE)], axis=0)
        return vn_st, u * _silu(gg)

    def gmlp_matmul(i, vn_st):
        return _dot_nt(vn_st, w_s_ref[i])

    def gmlp_epilogue(i, sv, ug):
        rows = slice(i * GM_GROUP_DIM, (i + 1) * GM_GROUP_DIM)
        sv = sv + b_s_ref[i:i + 1, :]
        svT = jnp.concatenate(
            [sv[c * GM_GROUP_DIM:(c + 1) * GM_GROUP_DIM] for c in range(BLOCKS_PER_TILE)],
            axis=1)
        ygT_ref[rows, :] = (ug * svT).astype(BF16)

    def unit_scores(blk, kvh, edge_lo, edge_hi):
        cols = slice(blk * BLOCK, (blk + 1) * BLOCK)
        heads = [kvh * Q_PER_KV + g for g in range(Q_PER_KV)]
        k_blk = k_ref[blk * BLOCK:blk * BLOCK + KEYS, :]
        q_cat = jnp.concatenate(
            [qT_old[hd * HEAD_DIM:(hd + 1) * HEAD_DIM, cols] for hd in heads], axis=1)
        zeros_q = jnp.zeros((HEAD_DIM, Q_PER_KV * BLOCK), BF16)
        rhs = jnp.concatenate([q_cat, zeros_q] if kvh == 0 else [zeros_q, q_cat], axis=0)
        s = jnp.dot(k_blk, rhs, preferred_element_type=F32)
        s = s + bias_ref[kvh]
        if blk == 0:
            s = jnp.concatenate([s[0:BLOCK] + edge_lo, s[BLOCK:]], axis=0)
        if blk == BLOCKS_PER_TILE - 1:
            s = jnp.concatenate([s[:2 * BLOCK], s[2 * BLOCK:] + edge_hi], axis=0)
        return s

    def unit_softmax(kvh, s):
        heads = [kvh * Q_PER_KV + g for g in range(Q_PER_KV)]
        sink_row = jnp.concatenate(
            [jnp.full((1, BLOCK), sink_ref[layer, hd] * LOG2E, F32) for hd in heads], axis=1)
        m = jnp.maximum(jnp.max(s, axis=0, keepdims=True), sink_row)
        p = jnp.exp2(s - m)
        l = jnp.sum(p, axis=0, keepdims=True) + jnp.exp2(sink_row - m)
        return p.astype(BF16), l

    def unit_pv(blk, kvh, p):
        vT_blk = vT_ref[kvh * HEAD_DIM:(kvh + 1) * HEAD_DIM,
                        blk * BLOCK:blk * BLOCK + KEYS]
        return jnp.dot(vT_blk, p, preferred_element_type=F32)

    def unit_epilogue(blk, kvh, o, l):
        cols = slice(blk * BLOCK, (blk + 1) * BLOCK)
        o = o * (1.0 / l)
        for g in range(Q_PER_KV):
            hd = kvh * Q_PER_KV + g
            rows = slice(hd * HEAD_DIM, (hd + 1) * HEAD_DIM)
            yaT_ref[rows, cols] = (o[:, g * BLOCK:(g + 1) * BLOCK] * sga_old[rows, cols]).astype(BF16)

    def step(project, attend):
        units = [(blk, kvh) for blk in range(BLOCKS_PER_TILE) for kvh in range(N_KV_HEADS)]
        assert len(units) == N_HEADS
        if attend:
            tile = j - 1
            edge_lo = jnp.where(tile == 0, NEG_INF, 0.0).astype(F32)
            edge_hi = jnp.where(tile == n_tiles - 1, NEG_INF, 0.0).astype(F32)
            o_ref[...] = xprev_ref[...] + gate * _dot_tn(ygT_ref[...], w_out_ref[D_ATTN:, :])
            s_next = unit_scores(*units[0], edge_lo, edge_hi)
        if project:
            norm_and_kv()
        pend_unit = None
        pend_gmlp = None
        for i in range(len(units) + 1):
            drain = i == len(units)
            if attend and not drain:
                s_cur = s_next
                if i + 1 < len(units):
                    s_next = unit_scores(*units[i + 1], edge_lo, edge_hi)
            if project and not drain:
                r = piece_matmul(i)
            if pend_unit is not None:
                o_prev = unit_pv(pend_unit[0], pend_unit[1], pend_unit[2])
            if pend_gmlp is not None:
                sv_prev = gmlp_matmul(pend_gmlp[0], pend_gmlp[1])
            new_gmlp = None
            if project and not drain:
                vn_st, ug = piece_epilogue(i, r)
                new_gmlp = (i, vn_st, ug)
            if pend_gmlp is not None:
                gmlp_epilogue(pend_gmlp[0], sv_prev, pend_gmlp[2])
            new_unit = None
            if attend and not drain:
                blk, kvh = units[i]
                p, l = unit_softmax(kvh, s_cur)
                new_unit = (blk, kvh, p, l)
            if pend_unit is not None:
                unit_epilogue(pend_unit[0], pend_unit[1], o_prev, pend_unit[3])
            pend_unit, pend_gmlp = new_unit, new_gmlp
        if attend:
            o_ref[...] += gate * _dot_tn(yaT_ref[...], w_out_ref[:D_ATTN, :])

    @pl.when(j == 1)
    def _no_halo():
        k_ref[0:BLOCK, :] = jnp.zeros((BLOCK, D_KV), BF16)
        vT_ref[:, 0:BLOCK] = jnp.zeros((D_KV, BLOCK), BF16)

    @pl.when(j > 1)
    def _halo():
        k_ref[0:BLOCK, :] = k_ref[T:T + BLOCK, :]
        vT_ref[:, 0:BLOCK] = vT_ref[:, T:T + BLOCK]

    @pl.when(j > 0)
    def _carry():
        k_ref[BLOCK:BLOCK + T, :] = k_ref[BLOCK + T:BLOCK + 2 * T, :]
        vT_ref[:, BLOCK:BLOCK + T] = vT_ref[:, BLOCK + T:BLOCK + 2 * T]
        qT_old[...] = qT_new[...]
        sga_old[...] = sga_new[...]

    pl.when(j == 0)(functools.partial(step, True, False))
    pl.when((j > 0) & (j < n_tiles))(functools.partial(step, True, True))
    pl.when(j == n_tiles)(functools.partial(step, False, True))


def _layer(layer, x, ada, norm_gain, w_in_t, q_gain, k_gain, sink, w_s, b_s, w_out):
    bsz, seq, _ = x.shape
    T = TILE
    n_tiles = seq // T
    lyr3 = lambda b, j: (layer, 0, 0)
    lyr4 = lambda b, j: (layer, 0, 0, 0)
    return pl.pallas_call(
        functools.partial(_layer_kernel, layer, n_tiles),
        grid=(bsz, n_tiles + 1),
        in_specs=[
            pl.BlockSpec(memory_space=pltpu.SMEM),
            pl.BlockSpec((None, T, D_MODEL), lambda b, j: (b, jnp.minimum(j, n_tiles - 1), 0)),
            pl.BlockSpec((None, T, D_MODEL), lambda b, j: (b, jnp.maximum(j - 1, 0), 0)),
            pl.BlockSpec((None, None, 1, 3 * D_MODEL), lambda b, j: (layer, b, 0, 0)),
            pl.BlockSpec((None, 1, D_MODEL), lyr3),
            pl.BlockSpec((None, D_IN, D_MODEL), lyr3),
            pl.BlockSpec((None, HEAD_DIM, 1), lyr3),
            pl.BlockSpec((None, HEAD_DIM, 1), lyr3),
            pl.BlockSpec((None, N_GM_GROUPS, BLOCK, BLOCK), lyr4),
            pl.BlockSpec((None, N_GM_GROUPS, BLOCK), lyr3),
            pl.BlockSpec((None, D_MODEL, D_MODEL), lyr3),
        ],
        out_specs=pl.BlockSpec((None, T, D_MODEL), lambda b, j: (b, jnp.maximum(j - 1, 0), 0)),
        out_shape=jax.ShapeDtypeStruct(x.shape, F32),
        scratch_shapes=[
            pltpu.VMEM((T, D_MODEL), BF16),
            pltpu.VMEM((D_ATTN, T), BF16),
            pltpu.VMEM((D_ATTN, T), BF16),
            pltpu.VMEM((D_ATTN, T), F32),
            pltpu.VMEM((D_ATTN, T), F32),
            pltpu.VMEM((D_ATTN, T), BF16),
            pltpu.VMEM((D_GM, T), BF16),
            pltpu.VMEM((BLOCK + 2 * T, D_KV), BF16),
            pltpu.VMEM((D_KV, BLOCK + 2 * T), BF16),
            pltpu.VMEM((N_KV_HEADS, KEYS, Q_PER_KV * BLOCK), F32),
        ],
        compiler_params=pltpu.CompilerParams(
            dimension_semantics=("arbitrary", "arbitrary"),
            vmem_limit_bytes=VMEM_LIMIT_BYTES),
        name="mixer_layer",
    )(sink, x, x, ada, norm_gain, w_in_t, q_gain, k_gain, w_s, b_s, w_out)


def kernel(x, c, w_ada, b_ada, norm_gain, w_in, q_gain, k_gain, sink, w_s, b_s, w_out):
    depth = w_in.shape[0]
    ada = _ada(c, w_ada, b_ada).reshape(depth, ADA_ROWS, 1, 3 * D_MODEL)
    w_in_t = jnp.swapaxes(w_in, 1, 2).astype(BF16)
    w_s_b = w_s.astype(BF16)
    w_out_b = w_out.astype(BF16)
    norm_gain = norm_gain.reshape(depth, 1, D_MODEL)
    q_gain = q_gain.reshape(depth, HEAD_DIM, 1)
    k_gain = k_gain.reshape(depth, HEAD_DIM, 1)
    for layer in range(depth):
        x = _layer(layer, x, ada, norm_gain, w_in_t, q_gain, k_gain, sink, w_s_b, b_s, w_out_b)
    return x
```

```python
import functools
import math

import jax
import jax.numpy as jnp
from jax import lax
from jax.experimental import pallas as pl
from jax.experimental.pallas import tpu as pltpu

D_MODEL = 1024
N_HEADS = 8
N_KV_HEADS = 2
Q_PER_KV = N_HEADS // N_KV_HEADS
HEAD_DIM = 64
D_ATTN = N_HEADS * HEAD_DIM
D_KV = N_KV_HEADS * HEAD_DIM
BLOCK = 128
N_GM_GROUPS = 8
GM_GROUP_DIM = 64
D_GM = N_GM_GROUPS * GM_GROUP_DIM
D_IN = D_ATTN + 2 * D_KV + D_ATTN + 3 * D_GM
EPS = 1e-6
NEG_INF = -1e30
LOG2E = math.log2(math.e)

Q0 = 0
K0 = Q0 + D_ATTN
V0 = K0 + D_KV
GA0 = V0 + D_KV
U0 = GA0 + D_ATTN
VG0 = U0 + D_GM
GG0 = VG0 + D_GM
assert GG0 + D_GM == D_IN and N_HEADS == N_GM_GROUPS and HEAD_DIM == GM_GROUP_DIM

TILE = 512
BLOCKS_PER_TILE = TILE // BLOCK
KEYS = 3 * BLOCK
ADA_ROWS = 8
VMEM_LIMIT_BYTES = 52 * 1024 * 1024

F32 = jnp.float32
BF16 = jnp.bfloat16


def _dot_nt(a, b):
    return lax.dot_general(a, b, (((1,), (1,)), ((), ())), preferred_element_type=F32)


def _dot_tn(a, b):
    return lax.dot_general(a, b, (((0,), (0,)), ((), ())), preferred_element_type=F32)


def _silu(v):
    return v * (1.0 / (1.0 + jnp.exp(-v)))


def _ada_kernel(c_ref, w_ref, b_ref, o_ref):
    cond = _silu(c_ref[...])
    o_ref[...] = jnp.dot(cond, w_ref[...], preferred_element_type=F32) + b_ref[...]


def _ada(c, w_ada, b_ada):
    depth = w_ada.shape[0]
    bsz = c.shape[0]
    c_pad = jnp.pad(c, ((0, ADA_ROWS - bsz), (0, 0)))
    tn = D_MODEL
    return pl.pallas_call(
        _ada_kernel,
        grid=(depth, 3 * D_MODEL // tn),
        in_specs=[
            pl.BlockSpec((ADA_ROWS, D_MODEL), lambda l, n: (0, 0)),
            pl.BlockSpec((None, D_MODEL, tn), lambda l, n: (l, 0, n)),
            pl.BlockSpec((None, 1, tn), lambda l, n: (l, 0, n)),
        ],
        out_specs=pl.BlockSpec((None, ADA_ROWS, tn), lambda l, n: (l, 0, n)),
        out_shape=jax.ShapeDtypeStruct((depth, ADA_ROWS, 3 * D_MODEL), F32),
        compiler_params=pltpu.CompilerParams(
            dimension_semantics=("arbitrary", "arbitrary")),
        name="ada",
    )(c_pad, w_ada, b_ada.reshape(depth, 1, 3 * D_MODEL))


def _layer_kernel(layer, n_tiles,
                  sink_ref, x_ref, ada_ref, ng_ref, w_in_t_ref, qg_ref, kg_ref,
                  w_s_ref, b_s_ref, w_out_ref,
                  o_ref,
                  h_ref, xres_ref, qT_new, qT_old, sga_new, sga_old, yaT_ref, ygT_ref, k_ref, vT_ref,
                  bias_ref):
    b = pl.program_id(0)
    j = pl.program_id(1)
    T = TILE

    @pl.when((b == 0) & (j == 0))
    def _init_bias():
        s_idx = lax.broadcasted_iota(jnp.int32, (KEYS, BLOCK), 0)
        q_idx = lax.broadcasted_iota(jnp.int32, (KEYS, BLOCK), 1)
        dist = jnp.abs(s_idx - BLOCK - q_idx).astype(F32)
        valid = dist <= float(BLOCK)
        for kvh in range(N_KV_HEADS):
            for g in range(Q_PER_KV):
                slope = 2.0 ** (-8.0 * (kvh * Q_PER_KV + g + 1) / N_HEADS)
                bias_ref[kvh, :, g * BLOCK:(g + 1) * BLOCK] = jnp.where(
                    valid, (-slope * LOG2E) * dist, NEG_INF)

    ada = ada_ref[...]
    shift = ada[:, 0:D_MODEL]
    scale = ada[:, D_MODEL:2 * D_MODEL]
    gate = ada[:, 2 * D_MODEL:3 * D_MODEL]

    def norm_and_kv():
        x = x_ref[...]
        xres_ref[...] = x
        ms = jnp.mean(x * x, axis=-1, keepdims=True)
        a = ng_ref[...] * (1.0 + scale)
        h_ref[...] = ((x * lax.rsqrt(ms + EPS)) * a + shift).astype(BF16)
        kvT = _dot_nt(w_in_t_ref[K0:GA0, :], h_ref[...])
        kn = []
        for kvh in range(N_KV_HEADS):
            kh = kvT[kvh * HEAD_DIM:(kvh + 1) * HEAD_DIM]
            ss = jnp.sum(kh * kh, axis=0, keepdims=True) * (1.0 / HEAD_DIM)
            kn.append(kh * lax.rsqrt(ss + EPS) * kg_ref[...])
        k_nat = jnp.concatenate(kn, axis=0).T
        k_ref[BLOCK + T:BLOCK + 2 * T, :] = k_nat.astype(BF16)
        vT_ref[:, BLOCK + T:BLOCK + 2 * T] = kvT[D_KV:2 * D_KV].astype(BF16)

    def piece_matmul(i):
        lhs = jnp.concatenate(
            [w_in_t_ref[off + i * HEAD_DIM:off + (i + 1) * HEAD_DIM, :]
             for off in (Q0, GA0, U0, VG0, GG0)], axis=0)
        return _dot_nt(lhs, h_ref[...])

    def piece_epilogue(i, r):
        rows = slice(i * HEAD_DIM, (i + 1) * HEAD_DIM)
        qh, ga, u, vg, gg = (r[n * HEAD_DIM:(n + 1) * HEAD_DIM] for n in range(5))
        ss = jnp.sum(qh * qh, axis=0, keepdims=True) * (1.0 / HEAD_DIM)
        rstd = lax.rsqrt(ss + EPS) * (LOG2E / math.sqrt(HEAD_DIM))
        qT_new[rows, :] = (qh * rstd * qg_ref[...]).astype(BF16)
        sga_new[rows, :] = _silu(ga)
        ss = jnp.sum(vg * vg, axis=0, keepdims=True) * (1.0 / GM_GROUP_DIM)
        vn = (vg * lax.rsqrt(ss + EPS)).astype(BF16)
        vn_st = jnp.concatenate(
            [vn[:, c * BLOCK:(c + 1) * BLOCK] for c in range(BLOCKS_PER_TILE)], axis=0)
        return vn_st, u * _silu(gg)

    def gmlp_matmul(i, vn_st):
        return _dot_nt(vn_st, w_s_ref[i])

    def gmlp_epilogue(i, sv, ug):
        rows = slice(i * GM_GROUP_DIM, (i + 1) * GM_GROUP_DIM)
        sv = sv + b_s_ref[i:i + 1, :]
        svT = jnp.concatenate(
            [sv[c * GM_GROUP_DIM:(c + 1) * GM_GROUP_DIM] for c in range(BLOCKS_PER_TILE)],
            axis=1)
        ygT_ref[rows, :] = (ug * svT).astype(BF16)

    def unit_scores(blk, kvh, edge_lo, edge_hi):
        cols = slice(blk * BLOCK, (blk + 1) * BLOCK)
        heads = [kvh * Q_PER_KV + g for g in range(Q_PER_KV)]
        k_blk = k_ref[blk * BLOCK:blk * BLOCK + KEYS, :]
        q_cat = jnp.concatenate(
            [qT_old[hd * HEAD_DIM:(hd + 1) * HEAD_DIM, cols] for hd in heads], axis=1)
        zeros_q = jnp.zeros((HEAD_DIM, Q_PER_KV * BLOCK), BF16)
        rhs = jnp.concatenate([q_cat, zeros_q] if kvh == 0 else [zeros_q, q_cat], axis=0)
        s = jnp.dot(k_blk, rhs, preferred_element_type=F32)
        s = s + bias_ref[kvh]
        if blk == 0:
            s = jnp.concatenate([s[0:BLOCK] + edge_lo, s[BLOCK:]], axis=0)
        if blk == BLOCKS_PER_TILE - 1:
            s = jnp.concatenate([s[:2 * BLOCK], s[2 * BLOCK:] + edge_hi], axis=0)
        return s

    def unit_softmax(kvh, s):
        heads = [kvh * Q_PER_KV + g for g in range(Q_PER_KV)]
        sink_row = jnp.concatenate(
            [jnp.full((1, BLOCK), sink_ref[layer, hd] * LOG2E, F32) for hd in heads], axis=1)
        m = jnp.maximum(jnp.max(s, axis=0, keepdims=True), sink_row)
        p = jnp.exp2(s - m)
        l = jnp.sum(p, axis=0, keepdims=True) + jnp.exp2(sink_row - m)
        return p.astype(BF16), l

    def unit_pv(blk, kvh, p):
        vT_blk = vT_ref[kvh * HEAD_DIM:(kvh + 1) * HEAD_DIM,
                        blk * BLOCK:blk * BLOCK + KEYS]
        return jnp.dot(vT_blk, p, preferred_element_type=F32)

    def unit_epilogue(blk, kvh, o, l):
        cols = slice(blk * BLOCK, (blk + 1) * BLOCK)
        o = o * (1.0 / l)
        for g in range(Q_PER_KV):
            hd = kvh * Q_PER_KV + g
            rows = slice(hd * HEAD_DIM, (hd + 1) * HEAD_DIM)
            yaT_ref[rows, cols] = (o[:, g * BLOCK:(g + 1) * BLOCK] * sga_old[rows, cols]).astype(BF16)

    def step(project, attend):
        units = [(blk, kvh) for blk in range(BLOCKS_PER_TILE) for kvh in range(N_KV_HEADS)]
        assert len(units) == N_HEADS
        if attend:
            tile = j - 1
            edge_lo = jnp.where(tile == 0, NEG_INF, 0.0).astype(F32)
            edge_hi = jnp.where(tile == n_tiles - 1, NEG_INF, 0.0).astype(F32)
            o_ref[...] = xres_ref[...] + gate * _dot_tn(ygT_ref[...], w_out_ref[D_ATTN:, :])
            s_next = unit_scores(*units[0], edge_lo, edge_hi)
        if project:
            norm_and_kv()
        pend_unit = None
        pend_gmlp = None
        for i in range(len(units) + 1):
            drain = i == len(units)
            if attend and not drain:
                s_cur = s_next
                if i + 1 < len(units):
                    s_next = unit_scores(*units[i + 1], edge_lo, edge_hi)
            if project and not drain:
                r = piece_matmul(i)
            if pend_unit is not None:
                o_prev = unit_pv(pend_unit[0], pend_unit[1], pend_unit[2])
            if pend_gmlp is not None:
                sv_prev = gmlp_matmul(pend_gmlp[0], pend_gmlp[1])
            new_gmlp = None
            if project and not drain:
                vn_st, ug = piece_epilogue(i, r)
                new_gmlp = (i, vn_st, ug)
            if pend_gmlp is not None:
                gmlp_epilogue(pend_gmlp[0], sv_prev, pend_gmlp[2])
            new_unit = None
            if attend and not drain:
                blk, kvh = units[i]
                p, l = unit_softmax(kvh, s_cur)
                new_unit = (blk, kvh, p, l)
            if pend_unit is not None:
                unit_epilogue(pend_unit[0], pend_unit[1], o_prev, pend_unit[3])
            pend_unit, pend_gmlp = new_unit, new_gmlp
        if attend:
            o_ref[...] += gate * _dot_tn(yaT_ref[...], w_out_ref[:D_ATTN, :])

    @pl.when(j == 1)
    def _no_halo():
        k_ref[0:BLOCK, :] = jnp.zeros((BLOCK, D_KV), BF16)
        vT_ref[:, 0:BLOCK] = jnp.zeros((D_KV, BLOCK), BF16)

    @pl.when(j > 1)
    def _halo():
        k_ref[0:BLOCK, :] = k_ref[T:T + BLOCK, :]
        vT_ref[:, 0:BLOCK] = vT_ref[:, T:T + BLOCK]

    @pl.when(j > 0)
    def _carry():
        k_ref[BLOCK:BLOCK + T, :] = k_ref[BLOCK + T:BLOCK + 2 * T, :]
        vT_ref[:, BLOCK:BLOCK + T] = vT_ref[:, BLOCK + T:BLOCK + 2 * T]
        qT_old[...] = qT_new[...]
        sga_old[...] = sga_new[...]

    pl.when(j == 0)(functools.partial(step, True, False))
    pl.when((j > 0) & (j < n_tiles))(functools.partial(step, True, True))
    pl.when(j == n_tiles)(functools.partial(step, False, True))


def _layer(layer, x, ada, norm_gain, w_in_t, q_gain, k_gain, sink, w_s, b_s, w_out):
    bsz, seq, _ = x.shape
    T = TILE
    n_tiles = seq // T
    lyr3 = lambda b, j: (layer, 0, 0)
    lyr4 = lambda b, j: (layer, 0, 0, 0)
    return pl.pallas_call(
        functools.partial(_layer_kernel, layer, n_tiles),
        grid=(bsz, n_tiles + 1),
        in_specs=[
            pl.BlockSpec(memory_space=pltpu.SMEM),
            pl.BlockSpec((None, T, D_MODEL), lambda b, j: (b, jnp.minimum(j, n_tiles - 1), 0)),
            pl.BlockSpec((None, None, 1, 3 * D_MODEL), lambda b, j: (layer, b, 0, 0)),
            pl.BlockSpec((None, 1, D_MODEL), lyr3),
            pl.BlockSpec((None, D_IN, D_MODEL), lyr3),
            pl.BlockSpec((None, HEAD_DIM, 1), lyr3),
            pl.BlockSpec((None, HEAD_DIM, 1), lyr3),
            pl.BlockSpec((None, N_GM_GROUPS, BLOCK, BLOCK), lyr4),
            pl.BlockSpec((None, N_GM_GROUPS, BLOCK), lyr3),
            pl.BlockSpec((None, D_MODEL, D_MODEL), lyr3),
        ],
        out_specs=pl.BlockSpec((None, T, D_MODEL), lambda b, j: (b, jnp.maximum(j - 1, 0), 0)),
        out_shape=jax.ShapeDtypeStruct(x.shape, F32),
        scratch_shapes=[
            pltpu.VMEM((T, D_MODEL), BF16),
            pltpu.VMEM((T, D_MODEL), F32),
            pltpu.VMEM((D_ATTN, T), BF16),
            pltpu.VMEM((D_ATTN, T), BF16),
            pltpu.VMEM((D_ATTN, T), F32),
            pltpu.VMEM((D_ATTN, T), F32),
            pltpu.VMEM((D_ATTN, T), BF16),
            pltpu.VMEM((D_GM, T), BF16),
            pltpu.VMEM((BLOCK + 2 * T, D_KV), BF16),
            pltpu.VMEM((D_KV, BLOCK + 2 * T), BF16),
            pltpu.VMEM((N_KV_HEADS, KEYS, Q_PER_KV * BLOCK), F32),
        ],
        compiler_params=pltpu.CompilerParams(
            dimension_semantics=("arbitrary", "arbitrary"),
            vmem_limit_bytes=VMEM_LIMIT_BYTES),
        name="mixer_layer",
    )(sink, x, ada, norm_gain, w_in_t, q_gain, k_gain, w_s, b_s, w_out)


def kernel(x, c, w_ada, b_ada, norm_gain, w_in, q_gain, k_gain, sink, w_s, b_s, w_out):
    depth = w_in.shape[0]
    ada = _ada(c, w_ada, b_ada).reshape(depth, ADA_ROWS, 1, 3 * D_MODEL)
    w_in_t = jnp.swapaxes(w_in, 1, 2).astype(BF16)
    w_s_b = w_s.astype(BF16)
    w_out_b = w_out.astype(BF16)
    norm_gain = norm_gain.reshape(depth, 1, D_MODEL)
    q_gain = q_gain.reshape(depth, HEAD_DIM, 1)
    k_gain = k_gain.reshape(depth, HEAD_DIM, 1)
    for layer in range(depth):
        x = _layer(layer, x, ada, norm_gain, w_in_t, q_gain, k_gain, sink, w_s_b, b_s, w_out_b)
    return x
```

```python
import functools
import math

import jax
import jax.numpy as jnp
from jax import lax
from jax.experimental import pallas as pl
from jax.experimental.pallas import tpu as pltpu

D_MODEL = 1024
N_HEADS = 8
N_KV_HEADS = 2
Q_PER_KV = N_HEADS // N_KV_HEADS
HEAD_DIM = 64
D_ATTN = N_HEADS * HEAD_DIM
D_KV = N_KV_HEADS * HEAD_DIM
BLOCK = 128
N_GM_GROUPS = 8
GM_GROUP_DIM = 64
D_GM = N_GM_GROUPS * GM_GROUP_DIM
D_IN = D_ATTN + 2 * D_KV + D_ATTN + 3 * D_GM
EPS = 1e-6
NEG_INF = -1e30
LOG2E = math.log2(math.e)

Q0 = 0
K0 = Q0 + D_ATTN
V0 = K0 + D_KV
GA0 = V0 + D_KV
U0 = GA0 + D_ATTN
VG0 = U0 + D_GM
GG0 = VG0 + D_GM
assert GG0 + D_GM == D_IN and N_HEADS == N_GM_GROUPS and HEAD_DIM == GM_GROUP_DIM

TILE = 1024
SUB = 512
BLOCKS_PER_TILE = TILE // BLOCK
SUBS_PER_TILE = TILE // SUB
CHUNKS_PER_SUB = SUB // BLOCK
KEYS = 3 * BLOCK
ADA_ROWS = 8
VMEM_LIMIT_BYTES = 58 * 1024 * 1024

F32 = jnp.float32
BF16 = jnp.bfloat16


def _dot_nt(a, b):
    return lax.dot_general(a, b, (((1,), (1,)), ((), ())), preferred_element_type=F32)


def _dot_tn(a, b):
    return lax.dot_general(a, b, (((0,), (0,)), ((), ())), preferred_element_type=F32)


def _silu(v):
    return v * (1.0 / (1.0 + jnp.exp(-v)))


def _ada_kernel(c_ref, w_ref, b_ref, o_ref):
    cond = _silu(c_ref[...])
    o_ref[...] = jnp.dot(cond, w_ref[...], preferred_element_type=F32) + b_ref[...]


def _ada(c, w_ada, b_ada):
    depth = w_ada.shape[0]
    bsz = c.shape[0]
    c_pad = jnp.pad(c, ((0, ADA_ROWS - bsz), (0, 0)))
    tn = D_MODEL
    return pl.pallas_call(
        _ada_kernel,
        grid=(depth, 3 * D_MODEL // tn),
        in_specs=[
            pl.BlockSpec((ADA_ROWS, D_MODEL), lambda l, n: (0, 0)),
            pl.BlockSpec((None, D_MODEL, tn), lambda l, n: (l, 0, n)),
            pl.BlockSpec((None, 1, tn), lambda l, n: (l, 0, n)),
        ],
        out_specs=pl.BlockSpec((None, ADA_ROWS, tn), lambda l, n: (l, 0, n)),
        out_shape=jax.ShapeDtypeStruct((depth, ADA_ROWS, 3 * D_MODEL), F32),
        compiler_params=pltpu.CompilerParams(
            dimension_semantics=("arbitrary", "arbitrary")),
        name="ada",
    )(c_pad, w_ada, b_ada.reshape(depth, 1, 3 * D_MODEL))


def _layer_kernel(layer, n_tiles,
                  sink_ref, x_ref, xprev_ref, ada_ref, ng_ref, w_in_t_ref, qg_ref, kg_ref,
                  w_s_ref, b_s_ref, w_out_ref,
                  o_ref,
                  h_ref, qT_new, qT_old, sga_new, sga_old, yaT_ref, ygT_ref, k_ref, vT_ref,
                  bias_ref):
    b = pl.program_id(0)
    j = pl.program_id(1)
    T = TILE

    @pl.when((b == 0) & (j == 0))
    def _init_bias():
        s_idx = lax.broadcasted_iota(jnp.int32, (KEYS, BLOCK), 0)
        q_idx = lax.broadcasted_iota(jnp.int32, (KEYS, BLOCK), 1)
        dist = jnp.abs(s_idx - BLOCK - q_idx).astype(F32)
        valid = dist <= float(BLOCK)
        for kvh in range(N_KV_HEADS):
            for g in range(Q_PER_KV):
                slope = 2.0 ** (-8.0 * (kvh * Q_PER_KV + g + 1) / N_HEADS)
                bias_ref[kvh, :, g * BLOCK:(g + 1) * BLOCK] = jnp.where(
                    valid, (-slope * LOG2E) * dist, NEG_INF)

    ada = ada_ref[...]
    shift = ada[:, 0:D_MODEL]
    scale = ada[:, D_MODEL:2 * D_MODEL]
    gate = ada[:, 2 * D_MODEL:3 * D_MODEL]

    def norm_and_kv():
        x = x_ref[...]
        ms = jnp.mean(x * x, axis=-1, keepdims=True)
        a = ng_ref[...] * (1.0 + scale)
        h_ref[...] = ((x * lax.rsqrt(ms + EPS)) * a + shift).astype(BF16)
        kvT = _dot_nt(w_in_t_ref[K0:GA0, :], h_ref[...])
        kn = []
        for kvh in range(N_KV_HEADS):
            kh = kvT[kvh * HEAD_DIM:(kvh + 1) * HEAD_DIM]
            ss = jnp.sum(kh * kh, axis=0, keepdims=True) * (1.0 / HEAD_DIM)
            kn.append(kh * lax.rsqrt(ss + EPS) * kg_ref[...])
        k_nat = jnp.concatenate(kn, axis=0).T
        k_ref[BLOCK + T:BLOCK + 2 * T, :] = k_nat.astype(BF16)
        vT_ref[:, BLOCK + T:BLOCK + 2 * T] = kvT[D_KV:2 * D_KV].astype(BF16)

    def piece_matmul(sub, i):
        lhs = jnp.concatenate(
            [w_in_t_ref[off + i * HEAD_DIM:off + (i + 1) * HEAD_DIM, :]
             for off in (Q0, GA0, U0, VG0, GG0)], axis=0)
        return _dot_nt(lhs, h_ref[sub * SUB:(sub + 1) * SUB, :])

    def piece_epilogue(sub, i, r):
        rows = slice(i * HEAD_DIM, (i + 1) * HEAD_DIM)
        cols = slice(sub * SUB, (sub + 1) * SUB)
        qh, ga, u, vg, gg = (r[n * HEAD_DIM:(n + 1) * HEAD_DIM] for n in range(5))
        ss = jnp.sum(qh * qh, axis=0, keepdims=True) * (1.0 / HEAD_DIM)
        rstd = lax.rsqrt(ss + EPS) * (LOG2E / math.sqrt(HEAD_DIM))
        qT_new[rows, cols] = (qh * rstd * qg_ref[...]).astype(BF16)
        sga_new[rows, cols] = _silu(ga)
        ss = jnp.sum(vg * vg, axis=0, keepdims=True) * (1.0 / GM_GROUP_DIM)
        vn = (vg * lax.rsqrt(ss + EPS)).astype(BF16)
        vn_st = jnp.concatenate(
            [vn[:, c * BLOCK:(c + 1) * BLOCK] for c in range(CHUNKS_PER_SUB)], axis=0)
        return vn_st, u * _silu(gg)

    def gmlp_matmul(i, vn_st):
        return _dot_nt(vn_st, w_s_ref[i])

    def gmlp_epilogue(sub, i, sv, ug):
        rows = slice(i * GM_GROUP_DIM, (i + 1) * GM_GROUP_DIM)
        sv = sv + b_s_ref[i:i + 1, :]
        svT = jnp.concatenate(
            [sv[c * GM_GROUP_DIM:(c + 1) * GM_GROUP_DIM] for c in range(CHUNKS_PER_SUB)],
            axis=1)
        ygT_ref[rows, sub * SUB:(sub + 1) * SUB] = (ug * svT).astype(BF16)

    def unit_scores(blk, kvh, edge_lo, edge_hi):
        cols = slice(blk * BLOCK, (blk + 1) * BLOCK)
        heads = [kvh * Q_PER_KV + g for g in range(Q_PER_KV)]
        k_blk = k_ref[blk * BLOCK:blk * BLOCK + KEYS, :]
        q_cat = jnp.concatenate(
            [qT_old[hd * HEAD_DIM:(hd + 1) * HEAD_DIM, cols] for hd in heads], axis=1)
        zeros_q = jnp.zeros((HEAD_DIM, Q_PER_KV * BLOCK), BF16)
        rhs = jnp.concatenate([q_cat, zeros_q] if kvh == 0 else [zeros_q, q_cat], axis=0)
        s = jnp.dot(k_blk, rhs, preferred_element_type=F32)
        s = s + bias_ref[kvh]
        if blk == 0:
            s = jnp.concatenate([s[0:BLOCK] + edge_lo, s[BLOCK:]], axis=0)
        if blk == BLOCKS_PER_TILE - 1:
            s = jnp.concatenate([s[:2 * BLOCK], s[2 * BLOCK:] + edge_hi], axis=0)
        return s

    def unit_softmax(kvh, s):
        heads = [kvh * Q_PER_KV + g for g in range(Q_PER_KV)]
        sink_row = jnp.concatenate(
            [jnp.full((1, BLOCK), sink_ref[layer, hd] * LOG2E, F32) for hd in heads], axis=1)
        m = jnp.maximum(jnp.max(s, axis=0, keepdims=True), sink_row)
        p = jnp.exp2(s - m)
        l = jnp.sum(p, axis=0, keepdims=True) + jnp.exp2(sink_row - m)
        return p.astype(BF16), l

    def unit_pv(blk, kvh, p):
        vT_blk = vT_ref[kvh * HEAD_DIM:(kvh + 1) * HEAD_DIM,
                        blk * BLOCK:blk * BLOCK + KEYS]
        return jnp.dot(vT_blk, p, preferred_element_type=F32)

    def unit_epilogue(blk, kvh, o, l):
        cols = slice(blk * BLOCK, (blk + 1) * BLOCK)
        o = o * (1.0 / l)
        for g in range(Q_PER_KV):
            hd = kvh * Q_PER_KV + g
            rows = slice(hd * HEAD_DIM, (hd + 1) * HEAD_DIM)
            yaT_ref[rows, cols] = (o[:, g * BLOCK:(g + 1) * BLOCK] * sga_old[rows, cols]).astype(BF16)

    def step(project, attend):
        units = [(blk, kvh) for blk in range(BLOCKS_PER_TILE) for kvh in range(N_KV_HEADS)]
        pieces = [(sub, i) for sub in range(SUBS_PER_TILE) for i in range(N_HEADS)]
        assert len(units) == len(pieces)
        if attend:
            tile = j - 1
            edge_lo = jnp.where(tile == 0, NEG_INF, 0.0).astype(F32)
            edge_hi = jnp.where(tile == n_tiles - 1, NEG_INF, 0.0).astype(F32)
            o_ref[...] = xprev_ref[...] + gate * _dot_tn(ygT_ref[...], w_out_ref[D_ATTN:, :])
            s_next = unit_scores(*units[0], edge_lo, edge_hi)
        if project:
            norm_and_kv()
        pend_unit = None
        pend_gmlp = None
        for i in range(len(units) + 1):
            drain = i == len(units)
            if attend and not drain:
                s_cur = s_next
                if i + 1 < len(units):
                    s_next = unit_scores(*units[i + 1], edge_lo, edge_hi)
            if project and not drain:
                r = piece_matmul(*pieces[i])
            if pend_unit is not None:
                o_prev = unit_pv(pend_unit[0], pend_unit[1], pend_unit[2])
            if pend_gmlp is not None:
                sv_prev = gmlp_matmul(pend_gmlp[1], pend_gmlp[2])
            new_gmlp = None
            if project and not drain:
                vn_st, ug = piece_epilogue(*pieces[i], r)
                new_gmlp = (*pieces[i], vn_st, ug)
            if pend_gmlp is not None:
                gmlp_epilogue(pend_gmlp[0], pend_gmlp[1], sv_prev, pend_gmlp[3])
            new_unit = None
            if attend and not drain:
                blk, kvh = units[i]
                p, l = unit_softmax(kvh, s_cur)
                new_unit = (blk, kvh, p, l)
            if pend_unit is not None:
                unit_epilogue(pend_unit[0], pend_unit[1], o_prev, pend_unit[3])
            pend_unit, pend_gmlp = new_unit, new_gmlp
        if attend:
            o_ref[...] += gate * _dot_tn(yaT_ref[...], w_out_ref[:D_ATTN, :])

    @pl.when(j == 1)
    def _no_halo():
        k_ref[0:BLOCK, :] = jnp.zeros((BLOCK, D_KV), BF16)
        vT_ref[:, 0:BLOCK] = jnp.zeros((D_KV, BLOCK), BF16)

    @pl.when(j > 1)
    def _halo():
        k_ref[0:BLOCK, :] = k_ref[T:T + BLOCK, :]
        vT_ref[:, 0:BLOCK] = vT_ref[:, T:T + BLOCK]

    @pl.when(j > 0)
    def _carry():
        k_ref[BLOCK:BLOCK + T, :] = k_ref[BLOCK + T:BLOCK + 2 * T, :]
        vT_ref[:, BLOCK:BLOCK + T] = vT_ref[:, BLOCK + T:BLOCK + 2 * T]
        qT_old[...] = qT_new[...]
        sga_old[...] = sga_new[...]

    pl.when(j == 0)(functools.partial(step, True, False))
    pl.when((j > 0) & (j < n_tiles))(functools.partial(step, True, True))
    pl.when(j == n_tiles)(functools.partial(step, False, True))


def _layer(layer, x, ada, norm_gain, w_in_t, q_gain, k_gain, sink, w_s, b_s, w_out):
    bsz, seq, _ = x.shape
    T = TILE
    n_tiles = seq // T
    lyr3 = lambda b, j: (layer, 0, 0)
    lyr4 = lambda b, j: (layer, 0, 0, 0)
    return pl.pallas_call(
        functools.partial(_layer_kernel, layer, n_tiles),
        grid=(bsz, n_tiles + 1),
        in_specs=[
            pl.BlockSpec(memory_space=pltpu.SMEM),
            pl.BlockSpec((None, T, D_MODEL), lambda b, j: (b, jnp.minimum(j, n_tiles - 1), 0)),
            pl.BlockSpec((None, T, D_MODEL), lambda b, j: (b, jnp.maximum(j - 1, 0), 0)),
            pl.BlockSpec((None, None, 1, 3 * D_MODEL), lambda b, j: (layer, b, 0, 0)),
            pl.BlockSpec((None, 1, D_MODEL), lyr3),
            pl.BlockSpec((None, D_IN, D_MODEL), lyr3),
            pl.BlockSpec((None, HEAD_DIM, 1), lyr3),
            pl.BlockSpec((None, HEAD_DIM, 1), lyr3),
            pl.BlockSpec((None, N_GM_GROUPS, BLOCK, BLOCK), lyr4),
            pl.BlockSpec((None, N_GM_GROUPS, BLOCK), lyr3),
            pl.BlockSpec((None, D_MODEL, D_MODEL), lyr3),
        ],
        out_specs=pl.BlockSpec((None, T, D_MODEL), lambda b, j: (b, jnp.maximum(j - 1, 0), 0)),
        out_shape=jax.ShapeDtypeStruct(x.shape, F32),
        scratch_shapes=[
            pltpu.VMEM((T, D_MODEL), BF16),
            pltpu.VMEM((D_ATTN, T), BF16),
            pltpu.VMEM((D_ATTN, T), BF16),
            pltpu.VMEM((D_ATTN, T), F32),
            pltpu.VMEM((D_ATTN, T), F32),
            pltpu.VMEM((D_ATTN, T), BF16),
            pltpu.VMEM((D_GM, T), BF16),
            pltpu.VMEM((BLOCK + 2 * T, D_KV), BF16),
            pltpu.VMEM((D_KV, BLOCK + 2 * T), BF16),
            pltpu.VMEM((N_KV_HEADS, KEYS, Q_PER_KV * BLOCK), F32),
        ],
        compiler_params=pltpu.CompilerParams(
            dimension_semantics=("arbitrary", "arbitrary"),
            vmem_limit_bytes=VMEM_LIMIT_BYTES),
        name="mixer_layer",
    )(sink, x, x, ada, norm_gain, w_in_t, q_gain, k_gain, w_s, b_s, w_out)


def kernel(x, c, w_ada, b_ada, norm_gain, w_in, q_gain, k_gain, sink, w_s, b_s, w_out):
    depth = w_in.shape[0]
    ada = _ada(c, w_ada, b_ada).reshape(depth, ADA_ROWS, 1, 3 * D_MODEL)
    w_in_t = jnp.swapaxes(w_in, 1, 2).astype(BF16)
    w_s_b = w_s.astype(BF16)
    w_out_b = w_out.astype(BF16)
    norm_gain = norm_gain.reshape(depth, 1, D_MODEL)
    q_gain = q_gain.reshape(depth, HEAD_DIM, 1)
    k_gain = k_gain.reshape(depth, HEAD_DIM, 1)
    for layer in range(depth):
        x = _layer(layer, x, ada, norm_gain, w_in_t, q_gain, k_gain, sink, w_s_b, b_s, w_out_b)
    return x
```

```python
import functools
import math

import jax
import jax.numpy as jnp
from jax import lax
from jax.experimental import pallas as pl
from jax.experimental.pallas import tpu as pltpu

D_MODEL = 1024
N_HEADS = 8
N_KV_HEADS = 2
Q_PER_KV = N_HEADS // N_KV_HEADS
HEAD_DIM = 64
D_ATTN = N_HEADS * HEAD_DIM
D_KV = N_KV_HEADS * HEAD_DIM
BLOCK = 128
N_GM_GROUPS = 8
GM_GROUP_DIM = 64
D_GM = N_GM_GROUPS * GM_GROUP_DIM
D_IN = D_ATTN + 2 * D_KV + D_ATTN + 3 * D_GM
EPS = 1e-6
NEG_INF = -1e30
LOG2E = math.log2(math.e)

Q0 = 0
K0 = Q0 + D_ATTN
V0 = K0 + D_KV
GA0 = V0 + D_KV
U0 = GA0 + D_ATTN
VG0 = U0 + D_GM
GG0 = VG0 + D_GM
assert GG0 + D_GM == D_IN and N_HEADS == N_GM_GROUPS and HEAD_DIM == GM_GROUP_DIM

TILE = 512
SUB = 512
BLOCKS_PER_TILE = TILE // BLOCK
SUBS_PER_TILE = TILE // SUB
CHUNKS_PER_SUB = SUB // BLOCK
KEYS = 3 * BLOCK
ADA_ROWS = 8
VMEM_LIMIT_BYTES = 58 * 1024 * 1024

F32 = jnp.float32
BF16 = jnp.bfloat16


def _dot_tn(a, b):
    return lax.dot_general(a, b, (((0,), (0,)), ((), ())), preferred_element_type=F32)


def _silu(v):
    return v * (1.0 / (1.0 + jnp.exp(-v)))


def _ada_kernel(c_ref, w_ref, b_ref, o_ref):
    cond = _silu(c_ref[...])
    o_ref[...] = jnp.dot(cond, w_ref[...], preferred_element_type=F32) + b_ref[...]


def _ada(c, w_ada, b_ada):
    depth = w_ada.shape[0]
    bsz = c.shape[0]
    c_pad = jnp.pad(c, ((0, ADA_ROWS - bsz), (0, 0)))
    tn = D_MODEL
    return pl.pallas_call(
        _ada_kernel,
        grid=(depth, 3 * D_MODEL // tn),
        in_specs=[
            pl.BlockSpec((ADA_ROWS, D_MODEL), lambda l, n: (0, 0)),
            pl.BlockSpec((None, D_MODEL, tn), lambda l, n: (l, 0, n)),
            pl.BlockSpec((None, 1, tn), lambda l, n: (l, 0, n)),
        ],
        out_specs=pl.BlockSpec((None, ADA_ROWS, tn), lambda l, n: (l, 0, n)),
        out_shape=jax.ShapeDtypeStruct((depth, ADA_ROWS, 3 * D_MODEL), F32),
        compiler_params=pltpu.CompilerParams(
            dimension_semantics=("arbitrary", "arbitrary")),
        name="ada",
    )(c_pad, w_ada, b_ada.reshape(depth, 1, 3 * D_MODEL))


def _layer_kernel(layer, n_tiles,
                  sink_ref, x_ref, xprev_ref, ada_ref, ng_ref, w_in_t_ref, qg_ref, kg_ref,
                  w_s_t_ref, b_s_ref, w_out_ref,
                  o_ref,
                  hT_ref, qT_new, qT_old, sga_new, sga_old, yaT_ref, ygT_ref, k_ref, vT_ref,
                  bias_ref):
    b = pl.program_id(0)
    j = pl.program_id(1)
    T = TILE

    @pl.when((b == 0) & (j == 0))
    def _init_bias():
        s_idx = lax.broadcasted_iota(jnp.int32, (KEYS, BLOCK), 0)
        q_idx = lax.broadcasted_iota(jnp.int32, (KEYS, BLOCK), 1)
        dist = jnp.abs(s_idx - BLOCK - q_idx).astype(F32)
        valid = dist <= float(BLOCK)
        for kvh in range(N_KV_HEADS):
            for g in range(Q_PER_KV):
                slope = 2.0 ** (-8.0 * (kvh * Q_PER_KV + g + 1) / N_HEADS)
                bias_ref[kvh, :, g * BLOCK:(g + 1) * BLOCK] = jnp.where(
                    valid, (-slope * LOG2E) * dist, NEG_INF)

    ada = ada_ref[...]
    shift = ada[:, 0:D_MODEL]
    scale = ada[:, D_MODEL:2 * D_MODEL]
    gate = ada[:, 2 * D_MODEL:3 * D_MODEL]

    def norm_and_kv():
        x = x_ref[...]
        ms = jnp.mean(x * x, axis=-1, keepdims=True)
        a = ng_ref[...] * (1.0 + scale)
        h = (x * lax.rsqrt(ms + EPS)) * a + shift
        hT_ref[...] = h.T.astype(BF16)
        kvT = jnp.dot(w_in_t_ref[K0:GA0, :], hT_ref[...],
                      preferred_element_type=F32)
        kn = []
        for kvh in range(N_KV_HEADS):
            kh = kvT[kvh * HEAD_DIM:(kvh + 1) * HEAD_DIM]
            ss = jnp.sum(kh * kh, axis=0, keepdims=True) * (1.0 / HEAD_DIM)
            kn.append(kh * lax.rsqrt(ss + EPS) * kg_ref[...])
        k_nat = jnp.concatenate(kn, axis=0).T
        k_ref[BLOCK + T:BLOCK + 2 * T, :] = k_nat.astype(BF16)
        vT_ref[:, BLOCK + T:BLOCK + 2 * T] = kvT[D_KV:2 * D_KV].astype(BF16)

    def piece_matmul(sub, i):
        lhs = jnp.concatenate(
            [w_in_t_ref[off + i * HEAD_DIM:off + (i + 1) * HEAD_DIM, :]
             for off in (Q0, GA0, U0, VG0, GG0)], axis=0)
        return jnp.dot(lhs, hT_ref[:, sub * SUB:(sub + 1) * SUB],
                       preferred_element_type=F32)

    def piece_epilogue(sub, i, r):
        rows = slice(i * HEAD_DIM, (i + 1) * HEAD_DIM)
        cols = slice(sub * SUB, (sub + 1) * SUB)
        qh, ga, u, vg, gg = (r[n * HEAD_DIM:(n + 1) * HEAD_DIM] for n in range(5))
        ss = jnp.sum(qh * qh, axis=0, keepdims=True) * (1.0 / HEAD_DIM)
        rstd = lax.rsqrt(ss + EPS) * (LOG2E / math.sqrt(HEAD_DIM))
        qT_new[rows, cols] = (qh * rstd * qg_ref[...]).astype(BF16)
        sga_new[rows, cols] = _silu(ga)
        ss = jnp.sum(vg * vg, axis=0, keepdims=True) * (1.0 / GM_GROUP_DIM)
        vn = (vg * lax.rsqrt(ss + EPS)).astype(BF16)
        vn_st = jnp.concatenate(
            [vn[:, c * BLOCK:(c + 1) * BLOCK] for c in range(CHUNKS_PER_SUB)], axis=0)
        return vn_st, u * _silu(gg)

    def gmlp_matmul(i, vn_st):
        return jnp.dot(vn_st, w_s_t_ref[i], preferred_element_type=F32)

    def gmlp_epilogue(sub, i, sv, ug):
        rows = slice(i * GM_GROUP_DIM, (i + 1) * GM_GROUP_DIM)
        sv = sv + b_s_ref[i:i + 1, :]
        svT = jnp.concatenate(
            [sv[c * GM_GROUP_DIM:(c + 1) * GM_GROUP_DIM] for c in range(CHUNKS_PER_SUB)],
            axis=1)
        ygT_ref[rows, sub * SUB:(sub + 1) * SUB] = (ug * svT).astype(BF16)

    def unit_scores(blk, kvh, edge_lo, edge_hi):
        cols = slice(blk * BLOCK, (blk + 1) * BLOCK)
        heads = [kvh * Q_PER_KV + g for g in range(Q_PER_KV)]
        k_blk = k_ref[blk * BLOCK:blk * BLOCK + KEYS, :]
        q_cat = jnp.concatenate(
            [qT_old[hd * HEAD_DIM:(hd + 1) * HEAD_DIM, cols] for hd in heads], axis=1)
        zeros_q = jnp.zeros((HEAD_DIM, Q_PER_KV * BLOCK), BF16)
        rhs = jnp.concatenate([q_cat, zeros_q] if kvh == 0 else [zeros_q, q_cat], axis=0)
        s = jnp.dot(k_blk, rhs, preferred_element_type=F32)
        s = s + bias_ref[kvh]
        if blk == 0:
            s = jnp.concatenate([s[0:BLOCK] + edge_lo, s[BLOCK:]], axis=0)
        if blk == BLOCKS_PER_TILE - 1:
            s = jnp.concatenate([s[:2 * BLOCK], s[2 * BLOCK:] + edge_hi], axis=0)
        return s

    def unit_softmax(kvh, s):
        heads = [kvh * Q_PER_KV + g for g in range(Q_PER_KV)]
        sink_row = jnp.concatenate(
            [jnp.full((1, BLOCK), sink_ref[layer, hd] * LOG2E, F32) for hd in heads], axis=1)
        m = jnp.maximum(jnp.max(s, axis=0, keepdims=True), sink_row)
        p = jnp.exp2(s - m)
        l = jnp.sum(p, axis=0, keepdims=True) + jnp.exp2(sink_row - m)
        return p.astype(BF16), l

    def unit_pv(blk, kvh, p):
        vT_blk = vT_ref[kvh * HEAD_DIM:(kvh + 1) * HEAD_DIM,
                        blk * BLOCK:blk * BLOCK + KEYS]
        return jnp.dot(vT_blk, p, preferred_element_type=F32)

    def unit_epilogue(blk, kvh, o, l):
        cols = slice(blk * BLOCK, (blk + 1) * BLOCK)
        o = o * (1.0 / l)
        for g in range(Q_PER_KV):
            hd = kvh * Q_PER_KV + g
            rows = slice(hd * HEAD_DIM, (hd + 1) * HEAD_DIM)
            yaT_ref[rows, cols] = (o[:, g * BLOCK:(g + 1) * BLOCK] * sga_old[rows, cols]).astype(BF16)

    def step(project, attend):
        units = [(blk, kvh) for blk in range(BLOCKS_PER_TILE) for kvh in range(N_KV_HEADS)]
        pieces = [(sub, i) for sub in range(SUBS_PER_TILE) for i in range(N_HEADS)]
        assert len(units) == len(pieces)
        if attend:
            tile = j - 1
            edge_lo = jnp.where(tile == 0, NEG_INF, 0.0).astype(F32)
            edge_hi = jnp.where(tile == n_tiles - 1, NEG_INF, 0.0).astype(F32)
            o_ref[...] = xprev_ref[...] + gate * _dot_tn(ygT_ref[...], w_out_ref[D_ATTN:, :])
            s_next = unit_scores(*units[0], edge_lo, edge_hi)
        if project:
            norm_and_kv()
        pend_unit = None
        pend_gmlp = None
        for i in range(len(units) + 1):
            drain = i == len(units)
            if attend and not drain:
                s_cur = s_next
                if i + 1 < len(units):
                    s_next = unit_scores(*units[i + 1], edge_lo, edge_hi)
            if project and not drain:
                r = piece_matmul(*pieces[i])
            if pend_unit is not None:
                o_prev = unit_pv(pend_unit[0], pend_unit[1], pend_unit[2])
            if pend_gmlp is not None:
                sv_prev = gmlp_matmul(pend_gmlp[1], pend_gmlp[2])
            new_gmlp = None
            if project and not drain:
                vn_st, ug = piece_epilogue(*pieces[i], r)
                new_gmlp = (*pieces[i], vn_st, ug)
            if pend_gmlp is not None:
                gmlp_epilogue(pend_gmlp[0], pend_gmlp[1], sv_prev, pend_gmlp[3])
            new_unit = None
            if attend and not drain:
                blk, kvh = units[i]
                p, l = unit_softmax(kvh, s_cur)
                new_unit = (blk, kvh, p, l)
            if pend_unit is not None:
                unit_epilogue(pend_unit[0], pend_unit[1], o_prev, pend_unit[3])
            pend_unit, pend_gmlp = new_unit, new_gmlp
        if attend:
            o_ref[...] += gate * _dot_tn(yaT_ref[...], w_out_ref[:D_ATTN, :])

    @pl.when(j == 1)
    def _no_halo():
        k_ref[0:BLOCK, :] = jnp.zeros((BLOCK, D_KV), BF16)
        vT_ref[:, 0:BLOCK] = jnp.zeros((D_KV, BLOCK), BF16)

    @pl.when(j > 1)
    def _halo():
        k_ref[0:BLOCK, :] = k_ref[T:T + BLOCK, :]
        vT_ref[:, 0:BLOCK] = vT_ref[:, T:T + BLOCK]

    @pl.when(j > 0)
    def _carry():
        k_ref[BLOCK:BLOCK + T, :] = k_ref[BLOCK + T:BLOCK + 2 * T, :]
        vT_ref[:, BLOCK:BLOCK + T] = vT_ref[:, BLOCK + T:BLOCK + 2 * T]
        qT_old[...] = qT_new[...]
        sga_old[...] = sga_new[...]

    pl.when(j == 0)(functools.partial(step, True, False))
    pl.when((j > 0) & (j < n_tiles))(functools.partial(step, True, True))
    pl.when(j == n_tiles)(functools.partial(step, False, True))


def _layer(layer, x, ada, norm_gain, w_in_t, q_gain, k_gain, sink, w_s_t, b_s, w_out):
    bsz, seq, _ = x.shape
    T = TILE
    n_tiles = seq // T
    lyr3 = lambda b, j: (layer, 0, 0)
    lyr4 = lambda b, j: (layer, 0, 0, 0)
    return pl.pallas_call(
        functools.partial(_layer_kernel, layer, n_tiles),
        grid=(bsz, n_tiles + 1),
        in_specs=[
            pl.BlockSpec(memory_space=pltpu.SMEM),
            pl.BlockSpec((None, T, D_MODEL), lambda b, j: (b, jnp.minimum(j, n_tiles - 1), 0)),
            pl.BlockSpec((None, T, D_MODEL), lambda b, j: (b, jnp.maximum(j - 1, 0), 0)),
            pl.BlockSpec((None, None, 1, 3 * D_MODEL), lambda b, j: (layer, b, 0, 0)),
            pl.BlockSpec((None, 1, D_MODEL), lyr3),
            pl.BlockSpec((None, D_IN, D_MODEL), lyr3),
            pl.BlockSpec((None, HEAD_DIM, 1), lyr3),
            pl.BlockSpec((None, HEAD_DIM, 1), lyr3),
            pl.BlockSpec((None, N_GM_GROUPS, BLOCK, BLOCK), lyr4),
            pl.BlockSpec((None, N_GM_GROUPS, BLOCK), lyr3),
            pl.BlockSpec((None, D_MODEL, D_MODEL), lyr3),
        ],
        out_specs=pl.BlockSpec((None, T, D_MODEL), lambda b, j: (b, jnp.maximum(j - 1, 0), 0)),
        out_shape=jax.ShapeDtypeStruct(x.shape, F32),
        scratch_shapes=[
            pltpu.VMEM((D_MODEL, T), BF16),
            pltpu.VMEM((D_ATTN, T), BF16),
            pltpu.VMEM((D_ATTN, T), BF16),
            pltpu.VMEM((D_ATTN, T), F32),
            pltpu.VMEM((D_ATTN, T), F32),
            pltpu.VMEM((D_ATTN, T), BF16),
            pltpu.VMEM((D_GM, T), BF16),
            pltpu.VMEM((BLOCK + 2 * T, D_KV), BF16),
            pltpu.VMEM((D_KV, BLOCK + 2 * T), BF16),
            pltpu.VMEM((N_KV_HEADS, KEYS, Q_PER_KV * BLOCK), F32),
        ],
        compiler_params=pltpu.CompilerParams(
            dimension_semantics=("arbitrary", "arbitrary"),
            vmem_limit_bytes=VMEM_LIMIT_BYTES),
        name="mixer_layer",
    )(sink, x, x, ada, norm_gain, w_in_t, q_gain, k_gain, w_s_t, b_s, w_out)


def kernel(x, c, w_ada, b_ada, norm_gain, w_in, q_gain, k_gain, sink, w_s, b_s, w_out):
    depth = w_in.shape[0]
    ada = _ada(c, w_ada, b_ada).reshape(depth, ADA_ROWS, 1, 3 * D_MODEL)
    w_in_t = jnp.swapaxes(w_in, 1, 2).astype(BF16)
    w_s_t = jnp.swapaxes(w_s, 2, 3).astype(BF16)
    w_out_b = w_out.astype(BF16)
    norm_gain = norm_gain.reshape(depth, 1, D_MODEL)
    q_gain = q_gain.reshape(depth, HEAD_DIM, 1)
    k_gain = k_gain.reshape(depth, HEAD_DIM, 1)
    for layer in range(depth):
        x = _layer(layer, x, ada, norm_gain, w_in_t, q_gain, k_gain, sink, w_s_t, b_s, w_out_b)
    return x
```

```python
import functools
import math

import jax
import jax.numpy as jnp
from jax import lax
from jax.experimental import pallas as pl
from jax.experimental.pallas import tpu as pltpu

D_MODEL = 1024
N_HEADS = 8
N_KV_HEADS = 2
Q_PER_KV = N_HEADS // N_KV_HEADS
HEAD_DIM = 64
D_ATTN = N_HEADS * HEAD_DIM
D_KV = N_KV_HEADS * HEAD_DIM
BLOCK = 128
N_GM_GROUPS = 8
GM_GROUP_DIM = 64
D_GM = N_GM_GROUPS * GM_GROUP_DIM
D_IN = D_ATTN + 2 * D_KV + D_ATTN + 3 * D_GM
EPS = 1e-6
NEG_INF = -1e30
LOG2E = math.log2(math.e)

Q0 = 0
K0 = Q0 + D_ATTN
V0 = K0 + D_KV
GA0 = V0 + D_KV
U0 = GA0 + D_ATTN
VG0 = U0 + D_GM
GG0 = VG0 + D_GM
assert GG0 + D_GM == D_IN and N_HEADS == N_GM_GROUPS and HEAD_DIM == GM_GROUP_DIM

TILE = 512
SUB = 512
BLOCKS_PER_TILE = TILE // BLOCK
SUBS_PER_TILE = TILE // SUB
CHUNKS_PER_SUB = SUB // BLOCK
KEYS = 3 * BLOCK
ADA_ROWS = 8
W_PREP_COLS = 256
assert D_IN % W_PREP_COLS == 0
VMEM_LIMIT_BYTES = 58 * 1024 * 1024

F32 = jnp.float32
BF16 = jnp.bfloat16


def _dot_tn(a, b):
    return lax.dot_general(a, b, (((0,), (0,)), ((), ())), preferred_element_type=F32)


def _silu(v):
    return v * (1.0 / (1.0 + jnp.exp(-v)))


def _ada_kernel(c_ref, w_ref, b_ref, o_ref):
    cond = _silu(c_ref[...])
    o_ref[...] = jnp.dot(cond, w_ref[...], preferred_element_type=F32) + b_ref[...]


def _ada(c, w_ada, b_ada):
    depth = w_ada.shape[0]
    bsz = c.shape[0]
    c_pad = jnp.pad(c, ((0, ADA_ROWS - bsz), (0, 0)))
    tn = D_MODEL
    return pl.pallas_call(
        _ada_kernel,
        grid=(depth, 3 * D_MODEL // tn),
        in_specs=[
            pl.BlockSpec((ADA_ROWS, D_MODEL), lambda l, n: (0, 0)),
            pl.BlockSpec((None, D_MODEL, tn), lambda l, n: (l, 0, n)),
            pl.BlockSpec((None, 1, tn), lambda l, n: (l, 0, n)),
        ],
        out_specs=pl.BlockSpec((None, ADA_ROWS, tn), lambda l, n: (l, 0, n)),
        out_shape=jax.ShapeDtypeStruct((depth, ADA_ROWS, 3 * D_MODEL), F32),
        compiler_params=pltpu.CompilerParams(
            dimension_semantics=("arbitrary", "arbitrary")),
        name="ada",
    )(c_pad, w_ada, b_ada.reshape(depth, 1, 3 * D_MODEL))


def _layer_kernel(layer, n_tiles,
                  sink_ref, x_ref, xprev_ref, ada_ref, ng_ref, w_in_f32_ref, qg_ref, kg_ref,
                  w_s_t_ref, b_s_ref, w_out_f32_ref,
                  o_ref,
                  w_in_t_ref, w_out_ref,
                  hT_ref, qT_new, qT_old, sga_new, sga_old, yaT_ref, ygT_ref, k_ref, vT_ref,
                  bias_ref):
    b = pl.program_id(0)
    j = pl.program_id(1)
    T = TILE

    @pl.when((b == 0) & (j == 0))
    def _init_weights_and_bias():
        for n0 in range(0, D_IN, W_PREP_COLS):
            w_in_t_ref[n0:n0 + W_PREP_COLS, :] = (
                w_in_f32_ref[:, n0:n0 + W_PREP_COLS].T.astype(BF16))
        w_out_ref[...] = w_out_f32_ref[...].astype(BF16)

        s_idx = lax.broadcasted_iota(jnp.int32, (KEYS, BLOCK), 0)
        q_idx = lax.broadcasted_iota(jnp.int32, (KEYS, BLOCK), 1)
        dist = jnp.abs(s_idx - BLOCK - q_idx).astype(F32)
        valid = dist <= float(BLOCK)
        for kvh in range(N_KV_HEADS):
            for g in range(Q_PER_KV):
                slope = 2.0 ** (-8.0 * (kvh * Q_PER_KV + g + 1) / N_HEADS)
                bias_ref[kvh, :, g * BLOCK:(g + 1) * BLOCK] = jnp.where(
                    valid, (-slope * LOG2E) * dist, NEG_INF)

    ada = ada_ref[...]
    shift = ada[:, 0:D_MODEL]
    scale = ada[:, D_MODEL:2 * D_MODEL]
    gate = ada[:, 2 * D_MODEL:3 * D_MODEL]

    def norm_and_kv():
        x = x_ref[...]
        ms = jnp.mean(x * x, axis=-1, keepdims=True)
        a = ng_ref[...] * (1.0 + scale)
        h = (x * lax.rsqrt(ms + EPS)) * a + shift
        hT_ref[...] = h.T.astype(BF16)
        kvT = jnp.dot(w_in_t_ref[K0:GA0, :], hT_ref[...],
                      preferred_element_type=F32)
        kn = []
        for kvh in range(N_KV_HEADS):
            kh = kvT[kvh * HEAD_DIM:(kvh + 1) * HEAD_DIM]
            ss = jnp.sum(kh * kh, axis=0, keepdims=True) * (1.0 / HEAD_DIM)
            kn.append(kh * lax.rsqrt(ss + EPS) * kg_ref[...])
        k_nat = jnp.concatenate(kn, axis=0).T
        k_ref[BLOCK + T:BLOCK + 2 * T, :] = k_nat.astype(BF16)
        vT_ref[:, BLOCK + T:BLOCK + 2 * T] = kvT[D_KV:2 * D_KV].astype(BF16)

    def piece_matmul(sub, i):
        lhs = jnp.concatenate(
            [w_in_t_ref[off + i * HEAD_DIM:off + (i + 1) * HEAD_DIM, :]
             for off in (Q0, GA0, U0, VG0, GG0)], axis=0)
        return jnp.dot(lhs, hT_ref[:, sub * SUB:(sub + 1) * SUB],
                       preferred_element_type=F32)

    def piece_epilogue(sub, i, r):
        rows = slice(i * HEAD_DIM, (i + 1) * HEAD_DIM)
        cols = slice(sub * SUB, (sub + 1) * SUB)
        qh, ga, u, vg, gg = (r[n * HEAD_DIM:(n + 1) * HEAD_DIM] for n in range(5))
        ss = jnp.sum(qh * qh, axis=0, keepdims=True) * (1.0 / HEAD_DIM)
        rstd = lax.rsqrt(ss + EPS) * (LOG2E / math.sqrt(HEAD_DIM))
        qT_new[rows, cols] = (qh * rstd * qg_ref[...]).astype(BF16)
        sga_new[rows, cols] = _silu(ga)
        ss = jnp.sum(vg * vg, axis=0, keepdims=True) * (1.0 / GM_GROUP_DIM)
        vn = (vg * lax.rsqrt(ss + EPS)).astype(BF16)
        vn_st = jnp.concatenate(
            [vn[:, c * BLOCK:(c + 1) * BLOCK] for c in range(CHUNKS_PER_SUB)], axis=0)
        return vn_st, u * _silu(gg)

    def gmlp_matmul(i, vn_st):
        return jnp.dot(vn_st, w_s_t_ref[i], preferred_element_type=F32)

    def gmlp_epilogue(sub, i, sv, ug):
        rows = slice(i * GM_GROUP_DIM, (i + 1) * GM_GROUP_DIM)
        sv = sv + b_s_ref[i:i + 1, :]
        svT = jnp.concatenate(
            [sv[c * GM_GROUP_DIM:(c + 1) * GM_GROUP_DIM] for c in range(CHUNKS_PER_SUB)],
            axis=1)
        ygT_ref[rows, sub * SUB:(sub + 1) * SUB] = (ug * svT).astype(BF16)

    def unit_scores(blk, kvh, edge_lo, edge_hi):
        cols = slice(blk * BLOCK, (blk + 1) * BLOCK)
        heads = [kvh * Q_PER_KV + g for g in range(Q_PER_KV)]
        k_blk = k_ref[blk * BLOCK:blk * BLOCK + KEYS, :]
        q_cat = jnp.concatenate(
            [qT_old[hd * HEAD_DIM:(hd + 1) * HEAD_DIM, cols] for hd in heads], axis=1)
        zeros_q = jnp.zeros((HEAD_DIM, Q_PER_KV * BLOCK), BF16)
        rhs = jnp.concatenate([q_cat, zeros_q] if kvh == 0 else [zeros_q, q_cat], axis=0)
        s = jnp.dot(k_blk, rhs, preferred_element_type=F32)
        s = s + bias_ref[kvh]
        if blk == 0:
            s = jnp.concatenate([s[0:BLOCK] + edge_lo, s[BLOCK:]], axis=0)
        if blk == BLOCKS_PER_TILE - 1:
            s = jnp.concatenate([s[:2 * BLOCK], s[2 * BLOCK:] + edge_hi], axis=0)
        return s

    def unit_softmax(kvh, s):
        heads = [kvh * Q_PER_KV + g for g in range(Q_PER_KV)]
        sink_row = jnp.concatenate(
            [jnp.full((1, BLOCK), sink_ref[layer, hd] * LOG2E, F32) for hd in heads], axis=1)
        m = jnp.maximum(jnp.max(s, axis=0, keepdims=True), sink_row)
        p = jnp.exp2(s - m)
        l = jnp.sum(p, axis=0, keepdims=True) + jnp.exp2(sink_row - m)
        return p.astype(BF16), l

    def unit_pv(blk, kvh, p):
        vT_blk = vT_ref[kvh * HEAD_DIM:(kvh + 1) * HEAD_DIM,
                        blk * BLOCK:blk * BLOCK + KEYS]
        return jnp.dot(vT_blk, p, preferred_element_type=F32)

    def unit_epilogue(blk, kvh, o, l):
        cols = slice(blk * BLOCK, (blk + 1) * BLOCK)
        o = o * (1.0 / l)
        for g in range(Q_PER_KV):
            hd = kvh * Q_PER_KV + g
            rows = slice(hd * HEAD_DIM, (hd + 1) * HEAD_DIM)
            yaT_ref[rows, cols] = (o[:, g * BLOCK:(g + 1) * BLOCK] * sga_old[rows, cols]).astype(BF16)

    def step(project, attend):
        units = [(blk, kvh) for blk in range(BLOCKS_PER_TILE) for kvh in range(N_KV_HEADS)]
        pieces = [(sub, i) for sub in range(SUBS_PER_TILE) for i in range(N_HEADS)]
        assert len(units) == len(pieces)
        if attend:
            tile = j - 1
            edge_lo = jnp.where(tile == 0, NEG_INF, 0.0).astype(F32)
            edge_hi = jnp.where(tile == n_tiles - 1, NEG_INF, 0.0).astype(F32)
            o_ref[...] = xprev_ref[...] + gate * _dot_tn(ygT_ref[...], w_out_ref[D_ATTN:, :])
            s_next = unit_scores(*units[0], edge_lo, edge_hi)
        if project:
            norm_and_kv()
        pend_unit = None
        pend_gmlp = None
        for i in range(len(units) + 1):
            drain = i == len(units)
            if attend and not drain:
                s_cur = s_next
                if i + 1 < len(units):
                    s_next = unit_scores(*units[i + 1], edge_lo, edge_hi)
            if project and not drain:
                r = piece_matmul(*pieces[i])
            if pend_unit is not None:
                o_prev = unit_pv(pend_unit[0], pend_unit[1], pend_unit[2])
            if pend_gmlp is not None:
                sv_prev = gmlp_matmul(pend_gmlp[1], pend_gmlp[2])
            new_gmlp = None
            if project and not drain:
                vn_st, ug = piece_epilogue(*pieces[i], r)
                new_gmlp = (*pieces[i], vn_st, ug)
            if pend_gmlp is not None:
                gmlp_epilogue(pend_gmlp[0], pend_gmlp[1], sv_prev, pend_gmlp[3])
            new_unit = None
            if attend and not drain:
                blk, kvh = units[i]
                p, l = unit_softmax(kvh, s_cur)
                new_unit = (blk, kvh, p, l)
            if pend_unit is not None:
                unit_epilogue(pend_unit[0], pend_unit[1], o_prev, pend_unit[3])
            pend_unit, pend_gmlp = new_unit, new_gmlp
        if attend:
            o_ref[...] += gate * _dot_tn(yaT_ref[...], w_out_ref[:D_ATTN, :])

    @pl.when(j == 1)
    def _no_halo():
        k_ref[0:BLOCK, :] = jnp.zeros((BLOCK, D_KV), BF16)
        vT_ref[:, 0:BLOCK] = jnp.zeros((D_KV, BLOCK), BF16)

    @pl.when(j > 1)
    def _halo():
        k_ref[0:BLOCK, :] = k_ref[T:T + BLOCK, :]
        vT_ref[:, 0:BLOCK] = vT_ref[:, T:T + BLOCK]

    @pl.when(j > 0)
    def _carry():
        k_ref[BLOCK:BLOCK + T, :] = k_ref[BLOCK + T:BLOCK + 2 * T, :]
        vT_ref[:, BLOCK:BLOCK + T] = vT_ref[:, BLOCK + T:BLOCK + 2 * T]
        qT_old[...] = qT_new[...]
        sga_old[...] = sga_new[...]

    pl.when(j == 0)(functools.partial(step, True, False))
    pl.when((j > 0) & (j < n_tiles))(functools.partial(step, True, True))
    pl.when(j == n_tiles)(functools.partial(step, False, True))


def _layer(layer, x, ada, norm_gain, w_in, q_gain, k_gain, sink, w_s_t, b_s, w_out):
    bsz, seq, _ = x.shape
    T = TILE
    n_tiles = seq // T
    lyr3 = lambda b, j: (layer, 0, 0)
    lyr4 = lambda b, j: (layer, 0, 0, 0)
    return pl.pallas_call(
        functools.partial(_layer_kernel, layer, n_tiles),
        grid=(bsz, n_tiles + 1),
        in_specs=[
            pl.BlockSpec(memory_space=pltpu.SMEM),
            pl.BlockSpec((None, T, D_MODEL), lambda b, j: (b, jnp.minimum(j, n_tiles - 1), 0)),
            pl.BlockSpec((None, T, D_MODEL), lambda b, j: (b, jnp.maximum(j - 1, 0), 0)),
            pl.BlockSpec((None, None, 1, 3 * D_MODEL), lambda b, j: (layer, b, 0, 0)),
            pl.BlockSpec((None, 1, D_MODEL), lyr3),
            pl.BlockSpec((None, D_MODEL, D_IN), lyr3),
            pl.BlockSpec((None, HEAD_DIM, 1), lyr3),
            pl.BlockSpec((None, HEAD_DIM, 1), lyr3),
            pl.BlockSpec((None, N_GM_GROUPS, BLOCK, BLOCK), lyr4),
            pl.BlockSpec((None, N_GM_GROUPS, BLOCK), lyr3),
            pl.BlockSpec((None, D_MODEL, D_MODEL), lyr3),
        ],
        out_specs=pl.BlockSpec((None, T, D_MODEL), lambda b, j: (b, jnp.maximum(j - 1, 0), 0)),
        out_shape=jax.ShapeDtypeStruct(x.shape, F32),
        scratch_shapes=[
            pltpu.VMEM((D_IN, D_MODEL), BF16),
            pltpu.VMEM((D_MODEL, D_MODEL), BF16),
            pltpu.VMEM((D_MODEL, T), BF16),
            pltpu.VMEM((D_ATTN, T), BF16),
            pltpu.VMEM((D_ATTN, T), BF16),
            pltpu.VMEM((D_ATTN, T), F32),
            pltpu.VMEM((D_ATTN, T), F32),
            pltpu.VMEM((D_ATTN, T), BF16),
            pltpu.VMEM((D_GM, T), BF16),
            pltpu.VMEM((BLOCK + 2 * T, D_KV), BF16),
            pltpu.VMEM((D_KV, BLOCK + 2 * T), BF16),
            pltpu.VMEM((N_KV_HEADS, KEYS, Q_PER_KV * BLOCK), F32),
        ],
        compiler_params=pltpu.CompilerParams(
            dimension_semantics=("arbitrary", "arbitrary"),
            vmem_limit_bytes=VMEM_LIMIT_BYTES),
        name="mixer_layer",
    )(sink, x, x, ada, norm_gain, w_in, q_gain, k_gain, w_s_t, b_s, w_out)


def kernel(x, c, w_ada, b_ada, norm_gain, w_in, q_gain, k_gain, sink, w_s, b_s, w_out):
    depth = w_in.shape[0]
    ada = _ada(c, w_ada, b_ada).reshape(depth, ADA_ROWS, 1, 3 * D_MODEL)
    w_s_t = jnp.swapaxes(w_s, 2, 3).astype(BF16)
    norm_gain = norm_gain.reshape(depth, 1, D_MODEL)
    q_gain = q_gain.reshape(depth, HEAD_DIM, 1)
    k_gain = k_gain.reshape(depth, HEAD_DIM, 1)
    for layer in range(depth):
        x = _layer(layer, x, ada, norm_gain, w_in, q_gain, k_gain, sink, w_s_t, b_s, w_out)
    return x
```

```python
import functools
import math

import jax
import jax.numpy as jnp
from jax import lax
from jax.experimental import pallas as pl
from jax.experimental.pallas import tpu as pltpu

D_MODEL = 1024
N_HEADS = 8
N_KV_HEADS = 2
Q_PER_KV = N_HEADS // N_KV_HEADS
HEAD_DIM = 64
D_ATTN = N_HEADS * HEAD_DIM
D_KV = N_KV_HEADS * HEAD_DIM
BLOCK = 128
N_GM_GROUPS = 8
GM_GROUP_DIM = 64
D_GM = N_GM_GROUPS * GM_GROUP_DIM
D_IN = D_ATTN + 2 * D_KV + D_ATTN + 3 * D_GM
EPS = 1e-6
NEG_INF = -1e30
LOG2E = math.log2(math.e)

Q0 = 0
K0 = Q0 + D_ATTN
V0 = K0 + D_KV
GA0 = V0 + D_KV
U0 = GA0 + D_ATTN
VG0 = U0 + D_GM
GG0 = VG0 + D_GM
assert GG0 + D_GM == D_IN and N_HEADS == N_GM_GROUPS and HEAD_DIM == GM_GROUP_DIM

TILE = 512
SUB = 512
BLOCKS_PER_TILE = TILE // BLOCK
SUBS_PER_TILE = TILE // SUB
CHUNKS_PER_SUB = SUB // BLOCK
KEYS = 3 * BLOCK
ADA_VMEM_LIMIT_BYTES = 40 * 1024 * 1024
W_PREP_COLS = 256
assert D_IN % W_PREP_COLS == 0
VMEM_LIMIT_BYTES = 58 * 1024 * 1024

F32 = jnp.float32
BF16 = jnp.bfloat16


def _dot_tn(a, b):
    return lax.dot_general(a, b, (((0,), (0,)), ((), ())), preferred_element_type=F32)


def _silu(v):
    return v * (1.0 / (1.0 + jnp.exp(-v)))


def _ada_kernel(c_ref, w_ref, b_ref, o_ref):
    cond = _silu(c_ref[...])
    bias = b_ref[pl.ds(pl.program_id(0), 1), :]
    o_ref[...] = jnp.dot(cond, w_ref[...], preferred_element_type=F32) + bias


def _ada(c, w_ada, b_ada):
    depth = w_ada.shape[0]
    bsz = c.shape[0]
    return pl.pallas_call(
        _ada_kernel,
        grid=(depth,),
        in_specs=[
            pl.BlockSpec((bsz, D_MODEL), lambda l: (0, 0)),
            pl.BlockSpec((None, D_MODEL, 3 * D_MODEL), lambda l: (l, 0, 0)),
            pl.BlockSpec((depth, 3 * D_MODEL), lambda l: (0, 0)),
        ],
        out_specs=pl.BlockSpec((None, bsz, 3 * D_MODEL), lambda l: (l, 0, 0)),
        out_shape=jax.ShapeDtypeStruct((depth, bsz, 3 * D_MODEL), F32),
        compiler_params=pltpu.CompilerParams(
            dimension_semantics=("arbitrary",),
            vmem_limit_bytes=ADA_VMEM_LIMIT_BYTES),
        name="ada",
    )(c, w_ada, b_ada)


def _layer_kernel(layer, n_tiles,
                  sink_ref, x_ref, xprev_ref, ada_ref, ng_ref, w_in_f32_ref, qg_ref, kg_ref,
                  w_s_f32_ref, b_s_ref, w_out_f32_ref,
                  o_ref,
                  w_in_t_ref, w_out_ref, w_s_t_ref,
                  hT_ref, qT_new, qT_old, sga_new, sga_old, yaT_ref, ygT_ref, k_ref, vT_ref,
                  bias_ref):
    b = pl.program_id(0)
    j = pl.program_id(1)
    T = TILE

    @pl.when((b == 0) & (j == 0))
    def _init_weights_and_bias():
        for n0 in range(0, D_IN, W_PREP_COLS):
            w_in_t_ref[n0:n0 + W_PREP_COLS, :] = (
                w_in_f32_ref[:, n0:n0 + W_PREP_COLS].T.astype(BF16))
        w_out_ref[...] = w_out_f32_ref[...].astype(BF16)
        for g in range(N_GM_GROUPS):
            w_s_t_ref[g] = w_s_f32_ref[g].T.astype(BF16)

        s_idx = lax.broadcasted_iota(jnp.int32, (KEYS, BLOCK), 0)
        q_idx = lax.broadcasted_iota(jnp.int32, (KEYS, BLOCK), 1)
        dist = jnp.abs(s_idx - BLOCK - q_idx).astype(F32)
        valid = dist <= float(BLOCK)
        for kvh in range(N_KV_HEADS):
            for g in range(Q_PER_KV):
                slope = 2.0 ** (-8.0 * (kvh * Q_PER_KV + g + 1) / N_HEADS)
                bias_ref[kvh, :, g * BLOCK:(g + 1) * BLOCK] = jnp.where(
                    valid, (-slope * LOG2E) * dist, NEG_INF)

    ada = ada_ref[pl.ds(b, 1), :]
    shift = ada[:, 0:D_MODEL]
    scale = ada[:, D_MODEL:2 * D_MODEL]
    gate = ada[:, 2 * D_MODEL:3 * D_MODEL]

    def norm_and_kv():
        x = x_ref[...]
        ms = jnp.mean(x * x, axis=-1, keepdims=True)
        a = ng_ref[layer:layer + 1, :] * (1.0 + scale)
        h = (x * lax.rsqrt(ms + EPS)) * a + shift
        hT_ref[...] = h.T.astype(BF16)
        kvT = jnp.dot(w_in_t_ref[K0:GA0, :], hT_ref[...],
                      preferred_element_type=F32)
        kn = []
        for kvh in range(N_KV_HEADS):
            kh = kvT[kvh * HEAD_DIM:(kvh + 1) * HEAD_DIM]
            ss = jnp.sum(kh * kh, axis=0, keepdims=True) * (1.0 / HEAD_DIM)
            kn.append(kh * lax.rsqrt(ss + EPS))
        gain = qg_ref[layer:layer + 1, :] * kg_ref[layer:layer + 1, :]
        k_nat = jnp.concatenate(kn, axis=0).T * jnp.concatenate([gain] * N_KV_HEADS, axis=1)
        k_ref[BLOCK + T:BLOCK + 2 * T, :] = k_nat.astype(BF16)
        vT_ref[:, BLOCK + T:BLOCK + 2 * T] = kvT[D_KV:2 * D_KV].astype(BF16)

    def piece_matmul(sub, i):
        lhs = jnp.concatenate(
            [w_in_t_ref[off + i * HEAD_DIM:off + (i + 1) * HEAD_DIM, :]
             for off in (Q0, GA0, U0, VG0, GG0)], axis=0)
        return jnp.dot(lhs, hT_ref[:, sub * SUB:(sub + 1) * SUB],
                       preferred_element_type=F32)

    def piece_epilogue(sub, i, r):
        rows = slice(i * HEAD_DIM, (i + 1) * HEAD_DIM)
        cols = slice(sub * SUB, (sub + 1) * SUB)
        qh, ga, u, vg, gg = (r[n * HEAD_DIM:(n + 1) * HEAD_DIM] for n in range(5))
        ss = jnp.sum(qh * qh, axis=0, keepdims=True) * (1.0 / HEAD_DIM)
        rstd = lax.rsqrt(ss + EPS) * (LOG2E / math.sqrt(HEAD_DIM))
        qT_new[rows, cols] = (qh * rstd).astype(BF16)
        sga_new[rows, cols] = _silu(ga)
        ss = jnp.sum(vg * vg, axis=0, keepdims=True) * (1.0 / GM_GROUP_DIM)
        vn = (vg * lax.rsqrt(ss + EPS)).astype(BF16)
        vn_st = jnp.concatenate(
            [vn[:, c * BLOCK:(c + 1) * BLOCK] for c in range(CHUNKS_PER_SUB)], axis=0)
        return vn_st, u * _silu(gg)

    def gmlp_matmul(i, vn_st):
        return jnp.dot(vn_st, w_s_t_ref[i], preferred_element_type=F32)

    def gmlp_epilogue(sub, i, sv, ug):
        rows = slice(i * GM_GROUP_DIM, (i + 1) * GM_GROUP_DIM)
        sv = sv + b_s_ref[i:i + 1, :]
        svT = jnp.concatenate(
            [sv[c * GM_GROUP_DIM:(c + 1) * GM_GROUP_DIM] for c in range(CHUNKS_PER_SUB)],
            axis=1)
        ygT_ref[rows, sub * SUB:(sub + 1) * SUB] = (ug * svT).astype(BF16)

    def unit_scores(blk, kvh, edge_lo, edge_hi):
        cols = slice(blk * BLOCK, (blk + 1) * BLOCK)
        heads = [kvh * Q_PER_KV + g for g in range(Q_PER_KV)]
        k_blk = k_ref[blk * BLOCK:blk * BLOCK + KEYS, :]
        q_cat = jnp.concatenate(
            [qT_old[hd * HEAD_DIM:(hd + 1) * HEAD_DIM, cols] for hd in heads], axis=1)
        zeros_q = jnp.zeros((HEAD_DIM, Q_PER_KV * BLOCK), BF16)
        rhs = jnp.concatenate([q_cat, zeros_q] if kvh == 0 else [zeros_q, q_cat], axis=0)
        s = jnp.dot(k_blk, rhs, preferred_element_type=F32)
        s = s + bias_ref[kvh]
        if blk == 0:
            s = jnp.concatenate([s[0:BLOCK] + edge_lo, s[BLOCK:]], axis=0)
        if blk == BLOCKS_PER_TILE - 1:
            s = jnp.concatenate([s[:2 * BLOCK], s[2 * BLOCK:] + edge_hi], axis=0)
        return s

    def unit_softmax(kvh, s):
        heads = [kvh * Q_PER_KV + g for g in range(Q_PER_KV)]
        sink_row = jnp.concatenate(
            [jnp.full((1, BLOCK), sink_ref[layer, hd] * LOG2E, F32) for hd in heads], axis=1)
        m = jnp.maximum(jnp.max(s, axis=0, keepdims=True), sink_row)
        p = jnp.exp2(s - m)
        l = jnp.sum(p, axis=0, keepdims=True) + jnp.exp2(sink_row - m)
        return p.astype(BF16), l

    def unit_pv(blk, kvh, p):
        vT_blk = vT_ref[kvh * HEAD_DIM:(kvh + 1) * HEAD_DIM,
                        blk * BLOCK:blk * BLOCK + KEYS]
        return jnp.dot(vT_blk, p, preferred_element_type=F32)

    def unit_epilogue(blk, kvh, o, l):
        cols = slice(blk * BLOCK, (blk + 1) * BLOCK)
        o = o * (1.0 / l)
        for g in range(Q_PER_KV):
            hd = kvh * Q_PER_KV + g
            rows = slice(hd * HEAD_DIM, (hd + 1) * HEAD_DIM)
            yaT_ref[rows, cols] = (o[:, g * BLOCK:(g + 1) * BLOCK] * sga_old[rows, cols]).astype(BF16)

    def step(project, attend):
        units = [(blk, kvh) for blk in range(BLOCKS_PER_TILE) for kvh in range(N_KV_HEADS)]
        pieces = [(sub, i) for sub in range(SUBS_PER_TILE) for i in range(N_HEADS)]
        assert len(units) == len(pieces)
        if attend:
            tile = j - 1
            edge_lo = jnp.where(tile == 0, NEG_INF, 0.0).astype(F32)
            edge_hi = jnp.where(tile == n_tiles - 1, NEG_INF, 0.0).astype(F32)
            o_ref[...] = xprev_ref[...] + gate * _dot_tn(ygT_ref[...], w_out_ref[D_ATTN:, :])
            s_next = unit_scores(*units[0], edge_lo, edge_hi)
        if project:
            norm_and_kv()
        pend_unit = None
        pend_gmlp = None
        for i in range(len(units) + 1):
            drain = i == len(units)
            if attend and not drain:
                s_cur = s_next
                if i + 1 < len(units):
                    s_next = unit_scores(*units[i + 1], edge_lo, edge_hi)
            if project and not drain:
                r = piece_matmul(*pieces[i])
            if pend_unit is not None:
                o_prev = unit_pv(pend_unit[0], pend_unit[1], pend_unit[2])
            if pend_gmlp is not None:
                sv_prev = gmlp_matmul(pend_gmlp[1], pend_gmlp[2])
            new_gmlp = None
            if project and not drain:
                vn_st, ug = piece_epilogue(*pieces[i], r)
                new_gmlp = (*pieces[i], vn_st, ug)
            if pend_gmlp is not None:
                gmlp_epilogue(pend_gmlp[0], pend_gmlp[1], sv_prev, pend_gmlp[3])
            new_unit = None
            if attend and not drain:
                blk, kvh = units[i]
                p, l = unit_softmax(kvh, s_cur)
                new_unit = (blk, kvh, p, l)
            if pend_unit is not None:
                unit_epilogue(pend_unit[0], pend_unit[1], o_prev, pend_unit[3])
            pend_unit, pend_gmlp = new_unit, new_gmlp
        if attend:
            o_ref[...] += gate * _dot_tn(yaT_ref[...], w_out_ref[:D_ATTN, :])

    @pl.when(j == 1)
    def _no_halo():
        k_ref[0:BLOCK, :] = jnp.zeros((BLOCK, D_KV), BF16)
        vT_ref[:, 0:BLOCK] = jnp.zeros((D_KV, BLOCK), BF16)

    @pl.when(j > 1)
    def _halo():
        k_ref[0:BLOCK, :] = k_ref[T:T + BLOCK, :]
        vT_ref[:, 0:BLOCK] = vT_ref[:, T:T + BLOCK]

    @pl.when(j > 0)
    def _carry():
        k_ref[BLOCK:BLOCK + T, :] = k_ref[BLOCK + T:BLOCK + 2 * T, :]
        vT_ref[:, BLOCK:BLOCK + T] = vT_ref[:, BLOCK + T:BLOCK + 2 * T]
        qT_old[...] = qT_new[...]
        sga_old[...] = sga_new[...]

    pl.when(j == 0)(functools.partial(step, True, False))
    pl.when((j > 0) & (j < n_tiles))(functools.partial(step, True, True))
    pl.when(j == n_tiles)(functools.partial(step, False, True))


def _layer(layer, x, ada, norm_gain, w_in, q_gain, k_gain, sink, w_s, b_s, w_out):
    bsz, seq, _ = x.shape
    T = TILE
    n_tiles = seq // T
    lyr3 = lambda b, j: (layer, 0, 0)
    lyr4 = lambda b, j: (layer, 0, 0, 0)
    return pl.pallas_call(
        functools.partial(_layer_kernel, layer, n_tiles),
        grid=(bsz, n_tiles + 1),
        in_specs=[
            pl.BlockSpec(memory_space=pltpu.SMEM),
            pl.BlockSpec((None, T, D_MODEL), lambda b, j: (b, jnp.minimum(j, n_tiles - 1), 0)),
            pl.BlockSpec((None, T, D_MODEL), lambda b, j: (b, jnp.maximum(j - 1, 0), 0)),
            pl.BlockSpec((None, bsz, 3 * D_MODEL), lyr3),
            pl.BlockSpec(norm_gain.shape, lambda b, j: (0, 0)),
            pl.BlockSpec((None, D_MODEL, D_IN), lyr3),
            pl.BlockSpec(q_gain.shape, lambda b, j: (0, 0)),
            pl.BlockSpec(k_gain.shape, lambda b, j: (0, 0)),
            pl.BlockSpec((None, N_GM_GROUPS, BLOCK, BLOCK), lyr4),
            pl.BlockSpec((None, N_GM_GROUPS, BLOCK), lyr3),
            pl.BlockSpec((None, D_MODEL, D_MODEL), lyr3),
        ],
        out_specs=pl.BlockSpec((None, T, D_MODEL), lambda b, j: (b, jnp.maximum(j - 1, 0), 0)),
        out_shape=jax.ShapeDtypeStruct(x.shape, F32),
        scratch_shapes=[
            pltpu.VMEM((D_IN, D_MODEL), BF16),
            pltpu.VMEM((D_MODEL, D_MODEL), BF16),
            pltpu.VMEM((N_GM_GROUPS, BLOCK, BLOCK), BF16),
            pltpu.VMEM((D_MODEL, T), BF16),
            pltpu.VMEM((D_ATTN, T), BF16),
            pltpu.VMEM((D_ATTN, T), BF16),
            pltpu.VMEM((D_ATTN, T), F32),
            pltpu.VMEM((D_ATTN, T), F32),
            pltpu.VMEM((D_ATTN, T), BF16),
            pltpu.VMEM((D_GM, T), BF16),
            pltpu.VMEM((BLOCK + 2 * T, D_KV), BF16),
            pltpu.VMEM((D_KV, BLOCK + 2 * T), BF16),
            pltpu.VMEM((N_KV_HEADS, KEYS, Q_PER_KV * BLOCK), F32),
        ],
        compiler_params=pltpu.CompilerParams(
            dimension_semantics=("arbitrary", "arbitrary"),
            vmem_limit_bytes=VMEM_LIMIT_BYTES),
        name="mixer_layer",
    )(sink, x, x, ada, norm_gain, w_in, q_gain, k_gain, w_s, b_s, w_out)


def kernel(x, c, w_ada, b_ada, norm_gain, w_in, q_gain, k_gain, sink, w_s, b_s, w_out):
    ada = _ada(c, w_ada, b_ada)
    for layer in range(w_in.shape[0]):
        x = _layer(layer, x, ada, norm_gain, w_in, q_gain, k_gain, sink, w_s, b_s, w_out)
    return x
```

```python
import functools
import math

import jax
import jax.numpy as jnp
from jax import lax
from jax.experimental import pallas as pl
from jax.experimental.pallas import tpu as pltpu

D_MODEL = 1024
N_HEADS = 8
N_KV_HEADS = 2
Q_PER_KV = N_HEADS // N_KV_HEADS
HEAD_DIM = 64
D_ATTN = N_HEADS * HEAD_DIM
D_KV = N_KV_HEADS * HEAD_DIM
BLOCK = 128
N_GM_GROUPS = 8
GM_GROUP_DIM = 64
D_GM = N_GM_GROUPS * GM_GROUP_DIM
D_IN = D_ATTN + 2 * D_KV + D_ATTN + 3 * D_GM
EPS = 1e-6
NEG_INF = -1e30
LOG2E = math.log2(math.e)

Q0 = 0
K0 = Q0 + D_ATTN
V0 = K0 + D_KV
GA0 = V0 + D_KV
U0 = GA0 + D_ATTN
VG0 = U0 + D_GM
GG0 = VG0 + D_GM
assert GG0 + D_GM == D_IN and N_HEADS == N_GM_GROUPS and HEAD_DIM == GM_GROUP_DIM

TILE = 512
SUB = 512
BLOCKS_PER_TILE = TILE // BLOCK
SUBS_PER_TILE = TILE // SUB
CHUNKS_PER_SUB = SUB // BLOCK
KEYS = 3 * BLOCK
ADA_VMEM_LIMIT_BYTES = 40 * 1024 * 1024
W_PREP_COLS = 256
assert D_IN % W_PREP_COLS == 0
VMEM_LIMIT_BYTES = 58 * 1024 * 1024

F32 = jnp.float32
BF16 = jnp.bfloat16


def _dot_tn(a, b):
    return lax.dot_general(a, b, (((0,), (0,)), ((), ())), preferred_element_type=F32)


def _silu(v):
    return v * (1.0 / (1.0 + jnp.exp(-v)))


def _ada_kernel(c_ref, w_ref, b_ref, o_ref):
    cond = _silu(c_ref[...])
    bias = b_ref[pl.ds(pl.program_id(0), 1), :]
    o_ref[...] = jnp.dot(cond, w_ref[...], preferred_element_type=F32) + bias


def _ada(c, w_ada, b_ada):
    depth = w_ada.shape[0]
    bsz = c.shape[0]
    return pl.pallas_call(
        _ada_kernel,
        grid=(depth,),
        in_specs=[
            pl.BlockSpec((bsz, D_MODEL), lambda l: (0, 0)),
            pl.BlockSpec((None, D_MODEL, 3 * D_MODEL), lambda l: (l, 0, 0)),
            pl.BlockSpec((depth, 3 * D_MODEL), lambda l: (0, 0)),
        ],
        out_specs=pl.BlockSpec((None, bsz, 3 * D_MODEL), lambda l: (l, 0, 0)),
        out_shape=jax.ShapeDtypeStruct((depth, bsz, 3 * D_MODEL), F32),
        compiler_params=pltpu.CompilerParams(
            dimension_semantics=("arbitrary",),
            vmem_limit_bytes=ADA_VMEM_LIMIT_BYTES),
        name="ada",
    )(c, w_ada, b_ada)


def _layer_kernel(layer, n_tiles,
                  sink_ref, x_ref, xprev_ref, ada_ref, ng_ref, w_in_f32_ref, qg_ref, kg_ref,
                  w_s_f32_ref, b_s_ref, w_out_f32_ref,
                  o_ref,
                  w_in_t_ref, w_out_ref, w_s_t_ref,
                  hT_ref, qT_buf, sga_buf, yaT_ref, ygT_ref, k_ref, vT_ref,
                  bias_ref):
    b = pl.program_id(0)
    j = pl.program_id(1)
    T = TILE
    new_slot = j % 2
    old_slot = 1 - new_slot

    @pl.when((b == 0) & (j == 0))
    def _init_weights_and_bias():
        for n0 in range(0, D_IN, W_PREP_COLS):
            w_in_t_ref[n0:n0 + W_PREP_COLS, :] = (
                w_in_f32_ref[:, n0:n0 + W_PREP_COLS].T.astype(BF16))
        w_out_ref[...] = w_out_f32_ref[...].astype(BF16)
        for g in range(N_GM_GROUPS):
            w_s_t_ref[g] = w_s_f32_ref[g].T.astype(BF16)

        s_idx = lax.broadcasted_iota(jnp.int32, (KEYS, BLOCK), 0)
        q_idx = lax.broadcasted_iota(jnp.int32, (KEYS, BLOCK), 1)
        dist = jnp.abs(s_idx - BLOCK - q_idx).astype(F32)
        valid = dist <= float(BLOCK)
        for kvh in range(N_KV_HEADS):
            for g in range(Q_PER_KV):
                slope = 2.0 ** (-8.0 * (kvh * Q_PER_KV + g + 1) / N_HEADS)
                bias_ref[kvh, :, g * BLOCK:(g + 1) * BLOCK] = jnp.where(
                    valid, (-slope * LOG2E) * dist, NEG_INF)

    ada = ada_ref[pl.ds(b, 1), :]
    shift = ada[:, 0:D_MODEL]
    scale = ada[:, D_MODEL:2 * D_MODEL]
    gate = ada[:, 2 * D_MODEL:3 * D_MODEL]

    def norm_and_kv():
        x = x_ref[...]
        ms = jnp.mean(x * x, axis=-1, keepdims=True)
        a = ng_ref[layer:layer + 1, :] * (1.0 + scale)
        h = (x * lax.rsqrt(ms + EPS)) * a + shift
        hT_ref[...] = h.T.astype(BF16)
        kvT = jnp.dot(w_in_t_ref[K0:GA0, :], hT_ref[...],
                      preferred_element_type=F32)
        kn = []
        for kvh in range(N_KV_HEADS):
            kh = kvT[kvh * HEAD_DIM:(kvh + 1) * HEAD_DIM]
            ss = jnp.sum(kh * kh, axis=0, keepdims=True) * (1.0 / HEAD_DIM)
            kn.append(kh * lax.rsqrt(ss + EPS))
        gain = qg_ref[layer:layer + 1, :] * kg_ref[layer:layer + 1, :]
        k_nat = jnp.concatenate(kn, axis=0).T * jnp.concatenate([gain] * N_KV_HEADS, axis=1)
        k_ref[BLOCK + T:BLOCK + 2 * T, :] = k_nat.astype(BF16)
        vT_ref[:, BLOCK + T:BLOCK + 2 * T] = kvT[D_KV:2 * D_KV].astype(BF16)

    def piece_matmul(sub, i):
        lhs = jnp.concatenate(
            [w_in_t_ref[off + i * HEAD_DIM:off + (i + 1) * HEAD_DIM, :]
             for off in (Q0, GA0, U0, VG0, GG0)], axis=0)
        return jnp.dot(lhs, hT_ref[:, sub * SUB:(sub + 1) * SUB],
                       preferred_element_type=F32)

    def piece_epilogue(sub, i, r):
        rows = slice(i * HEAD_DIM, (i + 1) * HEAD_DIM)
        cols = slice(sub * SUB, (sub + 1) * SUB)
        qh, ga, u, vg, gg = (r[n * HEAD_DIM:(n + 1) * HEAD_DIM] for n in range(5))
        ss = jnp.sum(qh * qh, axis=0, keepdims=True) * (1.0 / HEAD_DIM)
        rstd = lax.rsqrt(ss + EPS) * (LOG2E / math.sqrt(HEAD_DIM))
        qT_buf[new_slot, rows, cols] = (qh * rstd).astype(BF16)
        sga_buf[new_slot, rows, cols] = _silu(ga)
        ss = jnp.sum(vg * vg, axis=0, keepdims=True) * (1.0 / GM_GROUP_DIM)
        vn = (vg * lax.rsqrt(ss + EPS)).astype(BF16)
        vn_st = jnp.concatenate(
            [vn[:, c * BLOCK:(c + 1) * BLOCK] for c in range(CHUNKS_PER_SUB)], axis=0)
        return vn_st, u * _silu(gg)

    def gmlp_matmul(i, vn_st):
        return jnp.dot(vn_st, w_s_t_ref[i], preferred_element_type=F32)

    def gmlp_epilogue(sub, i, sv, ug):
        rows = slice(i * GM_GROUP_DIM, (i + 1) * GM_GROUP_DIM)
        sv = sv + b_s_ref[i:i + 1, :]
        svT = jnp.concatenate(
            [sv[c * GM_GROUP_DIM:(c + 1) * GM_GROUP_DIM] for c in range(CHUNKS_PER_SUB)],
            axis=1)
        ygT_ref[rows, sub * SUB:(sub + 1) * SUB] = (ug * svT).astype(BF16)

    def unit_scores(blk, kvh, edge_lo, edge_hi):
        cols = slice(blk * BLOCK, (blk + 1) * BLOCK)
        heads = [kvh * Q_PER_KV + g for g in range(Q_PER_KV)]
        k_blk = k_ref[blk * BLOCK:blk * BLOCK + KEYS, :]
        q_cat = jnp.concatenate(
            [qT_buf[old_slot, hd * HEAD_DIM:(hd + 1) * HEAD_DIM, cols] for hd in heads], axis=1)
        zeros_q = jnp.zeros((HEAD_DIM, Q_PER_KV * BLOCK), BF16)
        rhs = jnp.concatenate([q_cat, zeros_q] if kvh == 0 else [zeros_q, q_cat], axis=0)
        s = jnp.dot(k_blk, rhs, preferred_element_type=F32)
        s = s + bias_ref[kvh]
        if blk == 0:
            s = jnp.concatenate([s[0:BLOCK] + edge_lo, s[BLOCK:]], axis=0)
        if blk == BLOCKS_PER_TILE - 1:
            s = jnp.concatenate([s[:2 * BLOCK], s[2 * BLOCK:] + edge_hi], axis=0)
        return s

    def unit_softmax(kvh, s):
        heads = [kvh * Q_PER_KV + g for g in range(Q_PER_KV)]
        sink_row = jnp.concatenate(
            [jnp.full((1, BLOCK), sink_ref[layer, hd] * LOG2E, F32) for hd in heads], axis=1)
        m = jnp.maximum(jnp.max(s, axis=0, keepdims=True), sink_row)
        p = jnp.exp2(s - m)
        l = jnp.sum(p, axis=0, keepdims=True) + jnp.exp2(sink_row - m)
        return p.astype(BF16), l

    def unit_pv(blk, kvh, p):
        vT_blk = vT_ref[kvh * HEAD_DIM:(kvh + 1) * HEAD_DIM,
                        blk * BLOCK:blk * BLOCK + KEYS]
        return jnp.dot(vT_blk, p, preferred_element_type=F32)

    def unit_epilogue(blk, kvh, o, l):
        cols = slice(blk * BLOCK, (blk + 1) * BLOCK)
        o = o * (1.0 / l)
        for g in range(Q_PER_KV):
            hd = kvh * Q_PER_KV + g
            rows = slice(hd * HEAD_DIM, (hd + 1) * HEAD_DIM)
            gate_g = sga_buf[old_slot, rows, cols]
            yaT_ref[rows, cols] = (o[:, g * BLOCK:(g + 1) * BLOCK] * gate_g).astype(BF16)

    def step(project, attend):
        units = [(blk, kvh) for blk in range(BLOCKS_PER_TILE) for kvh in range(N_KV_HEADS)]
        pieces = [(sub, i) for sub in range(SUBS_PER_TILE) for i in range(N_HEADS)]
        assert len(units) == len(pieces)
        if attend:
            tile = j - 1
            edge_lo = jnp.where(tile == 0, NEG_INF, 0.0).astype(F32)
            edge_hi = jnp.where(tile == n_tiles - 1, NEG_INF, 0.0).astype(F32)
            o_ref[...] = xprev_ref[...] + gate * _dot_tn(ygT_ref[...], w_out_ref[D_ATTN:, :])
            s_next = unit_scores(*units[0], edge_lo, edge_hi)
        if project:
            norm_and_kv()
        pend_unit = None
        pend_gmlp = None
        for i in range(len(units) + 1):
            drain = i == len(units)
            if attend and not drain:
                s_cur = s_next
                if i + 1 < len(units):
                    s_next = unit_scores(*units[i + 1], edge_lo, edge_hi)
            if project and not drain:
                r = piece_matmul(*pieces[i])
            if pend_unit is not None:
                o_prev = unit_pv(pend_unit[0], pend_unit[1], pend_unit[2])
            if pend_gmlp is not None:
                sv_prev = gmlp_matmul(pend_gmlp[1], pend_gmlp[2])
            new_gmlp = None
            if project and not drain:
                vn_st, ug = piece_epilogue(*pieces[i], r)
                new_gmlp = (*pieces[i], vn_st, ug)
            if pend_gmlp is not None:
                gmlp_epilogue(pend_gmlp[0], pend_gmlp[1], sv_prev, pend_gmlp[3])
            new_unit = None
            if attend and not drain:
                blk, kvh = units[i]
                p, l = unit_softmax(kvh, s_cur)
                new_unit = (blk, kvh, p, l)
            if pend_unit is not None:
                unit_epilogue(pend_unit[0], pend_unit[1], o_prev, pend_unit[3])
            pend_unit, pend_gmlp = new_unit, new_gmlp
        if attend:
            o_ref[...] += gate * _dot_tn(yaT_ref[...], w_out_ref[:D_ATTN, :])

    @pl.when(j == 1)
    def _no_halo():
        k_ref[0:BLOCK, :] = jnp.zeros((BLOCK, D_KV), BF16)
        vT_ref[:, 0:BLOCK] = jnp.zeros((D_KV, BLOCK), BF16)

    @pl.when(j > 1)
    def _halo():
        k_ref[0:BLOCK, :] = k_ref[T:T + BLOCK, :]
        vT_ref[:, 0:BLOCK] = vT_ref[:, T:T + BLOCK]

    @pl.when(j > 0)
    def _carry():
        k_ref[BLOCK:BLOCK + T, :] = k_ref[BLOCK + T:BLOCK + 2 * T, :]
        vT_ref[:, BLOCK:BLOCK + T] = vT_ref[:, BLOCK + T:BLOCK + 2 * T]

    pl.when(j == 0)(functools.partial(step, True, False))
    pl.when((j > 0) & (j < n_tiles))(functools.partial(step, True, True))
    pl.when(j == n_tiles)(functools.partial(step, False, True))


def _layer(layer, x, ada, norm_gain, w_in, q_gain, k_gain, sink, w_s, b_s, w_out):
    bsz, seq, _ = x.shape
    T = TILE
    n_tiles = seq // T
    lyr3 = lambda b, j: (layer, 0, 0)
    lyr4 = lambda b, j: (layer, 0, 0, 0)
    return pl.pallas_call(
        functools.partial(_layer_kernel, layer, n_tiles),
        grid=(bsz, n_tiles + 1),
        in_specs=[
            pl.BlockSpec(memory_space=pltpu.SMEM),
            pl.BlockSpec((None, T, D_MODEL), lambda b, j: (b, jnp.minimum(j, n_tiles - 1), 0)),
            pl.BlockSpec((None, T, D_MODEL), lambda b, j: (b, jnp.maximum(j - 1, 0), 0)),
            pl.BlockSpec((None, bsz, 3 * D_MODEL), lyr3),
            pl.BlockSpec(norm_gain.shape, lambda b, j: (0, 0)),
            pl.BlockSpec((None, D_MODEL, D_IN), lyr3),
            pl.BlockSpec(q_gain.shape, lambda b, j: (0, 0)),
            pl.BlockSpec(k_gain.shape, lambda b, j: (0, 0)),
            pl.BlockSpec((None, N_GM_GROUPS, BLOCK, BLOCK), lyr4),
            pl.BlockSpec((None, N_GM_GROUPS, BLOCK), lyr3),
            pl.BlockSpec((None, D_MODEL, D_MODEL), lyr3),
        ],
        out_specs=pl.BlockSpec((None, T, D_MODEL), lambda b, j: (b, jnp.maximum(j - 1, 0), 0)),
        out_shape=jax.ShapeDtypeStruct(x.shape, F32),
        scratch_shapes=[
            pltpu.VMEM((D_IN, D_MODEL), BF16),
            pltpu.VMEM((D_MODEL, D_MODEL), BF16),
            pltpu.VMEM((N_GM_GROUPS, BLOCK, BLOCK), BF16),
            pltpu.VMEM((D_MODEL, T), BF16),
            pltpu.VMEM((2, D_ATTN, T), BF16),
            pltpu.VMEM((2, D_ATTN, T), F32),
            pltpu.VMEM((D_ATTN, T), BF16),
            pltpu.VMEM((D_GM, T), BF16),
            pltpu.VMEM((BLOCK + 2 * T, D_KV), BF16),
            pltpu.VMEM((D_KV, BLOCK + 2 * T), BF16),
            pltpu.VMEM((N_KV_HEADS, KEYS, Q_PER_KV * BLOCK), F32),
        ],
        compiler_params=pltpu.CompilerParams(
            dimension_semantics=("arbitrary", "arbitrary"),
            vmem_limit_bytes=VMEM_LIMIT_BYTES),
        name="mixer_layer",
    )(sink, x, x, ada, norm_gain, w_in, q_gain, k_gain, w_s, b_s, w_out)


def kernel(x, c, w_ada, b_ada, norm_gain, w_in, q_gain, k_gain, sink, w_s, b_s, w_out):
    ada = _ada(c, w_ada, b_ada)
    for layer in range(w_in.shape[0]):
        x = _layer(layer, x, ada, norm_gain, w_in, q_gain, k_gain, sink, w_s, b_s, w_out)
    return x
```
